```python
import math
import jax, jax.numpy as jnp
from jax import lax
import numpy as np

D_MODEL = 1024
BATCH = 8
SEQ = 2048
DEPTH = 1

N_META = 16
BLOCK = 128
PAD = BLOCK - N_META
HEAD_DIM = 64
N_HEADS_A = 8
N_KV_A = 2
GROUP_A = N_HEADS_A // N_KV_A
WINDOW = 128
N_HEADS_B = 8
T5_BUCKETS = 32
T5_MAX_DIST = 128
PEER_HEADS = 8
PEER_NKEYS = 128
PEER_EXPERTS = PEER_NKEYS * PEER_NKEYS
PEER_DQ = 256
PEER_TOPK = 16
PEER_CHUNK = 128
LN_EPS = 1e-5
NEG = -1e30
DEEPNORM_ALPHA = (2 * DEPTH) ** 0.25
DEEPNORM_BETA = (8 * DEPTH) ** -0.25
FORGET_BIAS = 3.0

W_QA = N_HEADS_A * HEAD_DIM
W_KA = N_KV_A * HEAD_DIM
W_B = N_HEADS_B * HEAD_DIM
IN_SIZES = (W_QA, W_KA, W_KA, W_B, W_B, W_B, N_HEADS_B, D_MODEL, D_MODEL)
IN_VALUE_COLS = (False, False, True, False, False, True, False, False, False)
IN_COLS = W_QA + 2 * W_KA + 3 * W_B + N_HEADS_B + 2 * D_MODEL

kernel_name = 'hybrid_swa_fox_peer_block'


def layer_norm(x, g, b):
    xf = x.astype(jnp.float32)
    mu = jnp.mean(xf, axis=-1, keepdims=True)
    xc = xf - mu
    var = jnp.mean(xc * xc, axis=-1, keepdims=True)
    y = xc * lax.rsqrt(var + LN_EPS) * g.astype(jnp.float32) + b.astype(jnp.float32)
    return y.astype(x.dtype)


def t5_bucket(n):
    max_exact = T5_BUCKETS // 2
    nf = jnp.maximum(n, 1).astype(jnp.float32)
    large = max_exact + (jnp.log(nf / max_exact) / math.log(T5_MAX_DIST / max_exact)
                         * (T5_BUCKETS - max_exact)).astype(jnp.int32)
    large = jnp.minimum(large, T5_BUCKETS - 1)
    return jnp.where(n < max_exact, n, large)


def sliding_window_attention(q, k, v, sinks, rel_table):
    B, P = q.shape[0], q.shape[1]
    nb = P // BLOCK
    qb = q.reshape(B, nb, BLOCK, N_KV_A, GROUP_A, HEAD_DIM)
    kb = k.reshape(B, nb, BLOCK, N_KV_A, HEAD_DIM)
    vb = v.reshape(B, nb, BLOCK, N_KV_A, HEAD_DIM)

    def with_prev(t):
        prev = jnp.pad(t, ((0, 0), (1, 0), (0, 0), (0, 0), (0, 0)))[:, :-1]
        return jnp.concatenate([prev, t], axis=2)

    kk, vv = with_prev(kb), with_prev(vb)
    s = jnp.einsum('bnqhgd,bnkhd->bnhgqk', qb, kk).astype(jnp.float32) * (HEAD_DIM ** -0.5)
    qi = jnp.arange(BLOCK)[:, None] + BLOCK
    kj = jnp.arange(2 * BLOCK)[None, :]
    dist = qi - kj
    bias = rel_table[t5_bucket(jnp.maximum(dist, 0))]
    bias = bias.reshape(BLOCK, 2 * BLOCK, N_KV_A, GROUP_A).transpose(2, 3, 0, 1).astype(jnp.float32)
    kabs = jnp.arange(nb)[:, None] * BLOCK - BLOCK + kj
    valid = (dist >= 0) & (dist < WINDOW) & (kabs[:, None, :] >= PAD)
    s = jnp.where(valid[None, :, None, None], s + bias, NEG)
    sink = jnp.broadcast_to(
        sinks.reshape(N_KV_A, GROUP_A)[None, None, :, :, None, None].astype(jnp.float32),
        s.shape[:-1] + (1,))
    p = jax.nn.softmax(jnp.concatenate([s, sink], axis=-1), axis=-1)[..., :-1]
    o = jnp.einsum('bnhgqk,bnkhd->bnqhgd', p.astype(v.dtype), vv)
    return o.reshape(B, P, N_HEADS_A * HEAD_DIM)


def forgetting_attention(q, k, v, log_f):
    B, P = q.shape[0], q.shape[1]
    nb = P // BLOCK
    c = jnp.cumsum(log_f, axis=1).transpose(0, 2, 1)
    outs = []
    for n in range(nb):
        q0, q1 = n * BLOCK, (n + 1) * BLOCK
        s = jnp.einsum('bqhd,bkhd->bhqk', q[:, q0:q1], k[:, :q1]).astype(jnp.float32) * (HEAD_DIM ** -0.5)
        s = s + (c[:, :, q0:q1, None] - c[:, :, None, :q1])
        qidx = jnp.arange(q0, q1)[:, None]
        kidx = jnp.arange(q1)[None, :]
        valid = (kidx <= qidx) & (kidx >= PAD)
        p = jax.nn.softmax(jnp.where(valid, s, NEG), axis=-1)
        outs.append(jnp.einsum('bhqk,bkhd->bqhd', p.astype(v.dtype), v[:, :q1]))
    return jnp.concatenate(outs, axis=1).reshape(B, P, N_HEADS_B * HEAD_DIM)


def mixer_sublayer(h, w_in, b_in, sinks, rel_table, w_pa, w_pb, w_o):
    B, L, _ = h.shape
    z = h @ w_in + b_in
    offs = np.cumsum(IN_SIZES)[:-1].tolist()
    qa, ka, va, qb, kb, vb, fb, ga, gb = jnp.split(z, offs, axis=-1)

    def heads(t, n_heads):
        t = jnp.pad(t, ((0, 0), (PAD, 0), (0, 0)))
        return t.reshape(B, PAD + L, n_heads, HEAD_DIM)

    log_f = jnp.pad(jax.nn.log_sigmoid(fb.astype(jnp.float32)), ((0, 0), (PAD, 0), (0, 0)))
    oa = sliding_window_attention(heads(qa, N_HEADS_A), heads(ka, N_KV_A), heads(va, N_KV_A),
                                  sinks, rel_table)[:, PAD:]
    ob = forgetting_attention(heads(qb, N_HEADS_B), heads(kb, N_HEADS_B), heads(vb, N_HEADS_B),
                              log_f)[:, PAD:]
    merged = jax.nn.sigmoid(ga) * (oa @ w_pa) + jax.nn.sigmoid(gb) * (ob @ w_pb)
    return merged @ w_o


def peer_ffn(h, w_q, sub_keys, u, v):
    B, L, D = h.shape
    T = B * L
    n_chunks = -(-T // PEER_CHUNK)
    xt = jnp.pad(h.reshape(T, D), ((0, n_chunks * PEER_CHUNK - T), (0, 0))).reshape(n_chunks, PEER_CHUNK, D)
    K = PEER_TOPK

    def chunk(xc):
        q = (xc @ w_q).reshape(PEER_CHUNK, PEER_HEADS, 2, PEER_DQ // 2)
        sc = jnp.einsum('chpd,hpkd->chpk', q, sub_keys).astype(jnp.float32)
        top_s, top_i = lax.top_k(sc, K)
        cand = top_s[:, :, 0, :, None] + top_s[:, :, 1, None, :]
        best_s, best_flat = lax.top_k(cand.reshape(PEER_CHUNK, PEER_HEADS, K * K), K)
        ia = jnp.take_along_axis(top_i[:, :, 0], best_flat // K, axis=-1)
        ib = jnp.take_along_axis(top_i[:, :, 1], best_flat % K, axis=-1)
        expert = ia * PEER_NKEYS + ib
        g = jax.nn.softmax(best_s, axis=-1).astype(xc.dtype)
        u_sel = u[expert]
        v_sel = v[expert]
        a = jax.nn.gelu(jnp.einsum('cd,chkd->chk', xc, u_sel), approximate=False) * g
        return jnp.einsum('chk,chkd->cd', a, v_sel)

    out = lax.map(chunk, xt)
    return out.reshape(n_chunks * PEER_CHUNK, D)[:T].reshape(B, L, D)


def setup_inputs(seed: int = 0) -> dict:
    key = jax.random.key(seed)
    ks = jax.random.split(key, 24)
    f32 = jnp.float32
    D = D_MODEL
    beta = DEEPNORM_BETA
    nrm = lambda k, shape, s: jax.random.normal(k, shape, f32) * s
    x = nrm(ks[0], (BATCH, SEQ, D), 1.0)
    meta_tokens = nrm(ks[1], (N_META, D), 1.0)
    emb_ln_g = 1.0 + nrm(ks[2], (D,), 0.02)
    emb_ln_b = nrm(ks[3], (D,), 0.02)
    rel_bias = nrm(ks[4], (T5_BUCKETS, N_HEADS_A), 0.5)
    in_keys = jax.random.split(ks[5], len(IN_SIZES))
    w_in = jnp.concatenate(
        [nrm(kk, (DEPTH, D, n), D ** -0.5 * (beta if isv else 1.0))
         for kk, n, isv in zip(in_keys, IN_SIZES, IN_VALUE_COLS)], axis=-1)
    b_in = nrm(ks[6], (DEPTH, IN_COLS), 0.02)
    f0 = W_QA + 2 * W_KA + 3 * W_B
    b_in = b_in.at[:, f0:f0 + N_HEADS_B].add(FORGET_BIAS)
    attn_sinks = nrm(ks[7], (DEPTH, N_HEADS_A), 0.5)
    w_pa = nrm(ks[8], (DEPTH, W_QA, D), W_QA ** -0.5 * beta)
    w_pb = nrm(ks[9], (DEPTH, W_B, D), W_B ** -0.5 * beta)
    w_o = nrm(ks[10], (DEPTH, D, D), D ** -0.5 * beta)
    ln1_g = 1.0 + nrm(ks[11], (DEPTH, D), 0.02)
    ln1_b = nrm(ks[12], (DEPTH, D), 0.02)
    peer_wq = nrm(ks[13], (DEPTH, D, PEER_HEADS * PEER_DQ), D ** -0.5)
    peer_subkeys = nrm(ks[14], (DEPTH, PEER_HEADS, 2, PEER_NKEYS, PEER_DQ // 2), (PEER_DQ // 2) ** -0.5)
    peer_u = nrm(ks[15], (DEPTH, PEER_EXPERTS, D), D ** -0.5 * beta)
    peer_v = nrm(ks[16], (DEPTH, PEER_EXPERTS, D), beta)
    ln2_g = 1.0 + nrm(ks[17], (DEPTH, D), 0.02)
    ln2_b = nrm(ks[18], (DEPTH, D), 0.02)
    return {'x': x, 'meta_tokens': meta_tokens, 'emb_ln_g': emb_ln_g, 'emb_ln_b': emb_ln_b,
            'rel_bias': rel_bias, 'w_in': w_in, 'b_in': b_in, 'attn_sinks': attn_sinks,
            'w_pa': w_pa, 'w_pb': w_pb, 'w_o': w_o, 'ln1_g': ln1_g, 'ln1_b': ln1_b,
            'peer_wq': peer_wq, 'peer_subkeys': peer_subkeys, 'peer_u': peer_u, 'peer_v': peer_v,
            'ln2_g': ln2_g, 'ln2_b': ln2_b}


def reference(x, meta_tokens, emb_ln_g, emb_ln_b, rel_bias, w_in, b_in, attn_sinks, w_pa, w_pb, w_o,
              ln1_g, ln1_b, peer_wq, peer_subkeys, peer_u, peer_v, ln2_g, ln2_b):
    B = x.shape[0]
    meta = jnp.broadcast_to(meta_tokens[None].astype(x.dtype), (B, N_META, D_MODEL))
    h = layer_norm(jnp.concatenate([meta, x], axis=1), emb_ln_g, emb_ln_b)
    for l in range(DEPTH):
        y = mixer_sublayer(h, w_in[l], b_in[l], attn_sinks[l], rel_bias, w_pa[l], w_pb[l], w_o[l])
        h = layer_norm(DEEPNORM_ALPHA * h + y, ln1_g[l], ln1_b[l])
        y = peer_ffn(h, peer_wq[l], peer_subkeys[l], peer_u[l], peer_v[l])
        h = layer_norm(DEEPNORM_ALPHA * h + y, ln2_g[l], ln2_b[l])
    return h[:, N_META:]
```

```python
import functools
import math

import jax
import jax.numpy as jnp
import numpy as np
from jax import lax
from jax.experimental import pallas as pl
from jax.experimental.pallas import tpu as pltpu

N_META = 16
BLOCK = 128
PAD = BLOCK - N_META
HEAD_DIM = 64
N_HEADS_A = 8
N_KV_A = 2
GROUP_A = N_HEADS_A // N_KV_A
WINDOW = 128
N_HEADS_B = 8
T5_BUCKETS = 32
T5_MAX_DIST = 128
PEER_HEADS = 8
PEER_NKEYS = 128
PEER_DQ = 256
PEER_TOPK = 16
LN_EPS = 1e-5
NEG = -1e30
DEPTH = 1
DEEPNORM_ALPHA = (2 * DEPTH) ** 0.25

W_QA = N_HEADS_A * HEAD_DIM
W_KA = N_KV_A * HEAD_DIM
W_B = N_HEADS_B * HEAD_DIM

F32 = jnp.float32
BF16 = jnp.bfloat16

VMEM_LIMIT = 56 * 1024 * 1024


def _ln(x, g, b):
    mu = jnp.mean(x, axis=-1, keepdims=True)
    xc = x - mu
    var = jnp.mean(xc * xc, axis=-1, keepdims=True)
    return xc * lax.rsqrt(var + LN_EPS) * g + b


def _dot(a, b):
    return jnp.dot(a, b, preferred_element_type=F32)


def _dot_nt(a, b):
    return lax.dot_general(a, b, (((1,), (1,)), ((), ())), preferred_element_type=F32)


def _dot_tn(a, b):
    return lax.dot_general(a, b, (((0,), (0,)), ((), ())), preferred_element_type=F32)


def _split3(x):
    hi = x.astype(BF16)
    r = x - hi.astype(F32)
    mid = r.astype(BF16)
    lo = (r - mid.astype(F32)).astype(BF16)
    return hi, mid, lo


QT_COLS = 512


def _inproj_kernel(x_ref, meta_ref, g0_ref, b0_ref, wkv_ref, bkv_ref, wkbt_ref, bkbt_ref,
                   wfbt_ref, bfbt_ref, wq_ref, bq_ref,
                   kva_ref, vb_ref, kbt_ref, c_ref, zq_ref, hs_ref):
    j = pl.program_id(1)
    P = hs_ref.shape[0]

    @pl.when(j == 0)
    def _():
        g0 = g0_ref[...]
        b0 = b0_ref[...]
        hs_ref[0:PAD, :] = jnp.zeros((PAD, hs_ref.shape[1]), BF16)
        hs_ref[PAD:BLOCK, :] = _ln(meta_ref[...], g0, b0).astype(BF16)
        hs_ref[BLOCK:, :] = _ln(x_ref[0], g0, b0).astype(BF16)
        hs = hs_ref[...]
        nkv = 2 * W_KA
        kva_ref[0] = (_dot(hs, wkv_ref[:, :nkv]) + bkv_ref[:, :nkv]).astype(BF16)
        vb_ref[0, :P, :] = (_dot(hs, wkv_ref[:, nkv:]) + bkv_ref[:, nkv:]).astype(BF16)
        vb_ref[0, P:, :] = jnp.zeros((vb_ref.shape[1] - P, vb_ref.shape[2]), BF16)
        kbt_ref[0, :, :P] = (_dot_nt(wkbt_ref[...], hs) + bkbt_ref[...]).astype(BF16)
        kbt_ref[0, :, P:] = jnp.zeros((kbt_ref.shape[1], kbt_ref.shape[2] - P), BF16)
        c_ref[0, :, P:] = jnp.zeros((c_ref.shape[1], c_ref.shape[2] - P), F32)
        fbt = _dot_nt(wfbt_ref[...], hs) + bfbt_ref[...]
        lf = jax.nn.log_sigmoid(fbt)
        pos = lax.broadcasted_iota(jnp.int32, lf.shape, 1)
        lf = jnp.where(pos >= PAD, lf, 0.0)
        ri = lax.broadcasted_iota(jnp.int32, (BLOCK, BLOCK), 0)
        ci = lax.broadcasted_iota(jnp.int32, (BLOCK, BLOCK), 1)
        tri = jnp.where(ri <= ci, 1.0, 0.0).astype(BF16)
        carry = jnp.zeros((lf.shape[0], 1), F32)
        for n in range(P // BLOCK):
            hi, mid, lo = _split3(lf[:, n * BLOCK:(n + 1) * BLOCK])
            cb = _dot(hi, tri) + _dot(mid, tri) + _dot(lo, tri) + carry
            c_ref[0, :, n * BLOCK:(n + 1) * BLOCK] = cb[:N_HEADS_B]
            carry = cb[:, BLOCK - 1:BLOCK]

    @pl.when(j > 0)
    def _():
        zq_ref[0] = (_dot(hs_ref[BLOCK:, :], wq_ref[...]) + bq_ref[...]).astype(BF16)


def _inproj(x, meta, g0, b0, wkv, bkv, wkbt, bkbt, wfbt, bfbt, wq, bq):
    B, S, D = x.shape
    P = S + BLOCK
    P2 = -(-P // FOX_TK) * FOX_TK
    nq = wq.shape[1] // QT_COLS
    full = lambda shape: pl.BlockSpec(shape, lambda b, j: (0,) * len(shape))
    qcol = lambda b, j: (0, jnp.maximum(j - 1, 0))
    return pl.pallas_call(
        _inproj_kernel,
        grid=(B, nq + 1),
        in_specs=[
            pl.BlockSpec((1, S, D), lambda b, j: (b, 0, 0)),
            full(meta.shape), full(g0.shape), full(b0.shape),
            full(wkv.shape), full(bkv.shape), full(wkbt.shape), full(bkbt.shape),
            full(wfbt.shape), full(bfbt.shape),
            pl.BlockSpec((D, QT_COLS), qcol),
            pl.BlockSpec((1, QT_COLS), qcol),
        ],
        out_specs=[
            pl.BlockSpec((1, P, 2 * W_KA), lambda b, j: (b, 0, 0)),
            pl.BlockSpec((1, P2, W_B), lambda b, j: (b, 0, 0)),
            pl.BlockSpec((1, W_B, P2), lambda b, j: (b, 0, 0)),
            pl.BlockSpec((1, N_HEADS_B, P2), lambda b, j: (b, 0, 0)),
            pl.BlockSpec((1, S, QT_COLS), lambda b, j: (b, 0, jnp.maximum(j - 1, 0))),
        ],
        out_shape=[
            jax.ShapeDtypeStruct((B, P, 2 * W_KA), BF16),
            jax.ShapeDtypeStruct((B, P2, W_B), BF16),
            jax.ShapeDtypeStruct((B, W_B, P2), BF16),
            jax.ShapeDtypeStruct((B, N_HEADS_B, P2), F32),
            jax.ShapeDtypeStruct((B, S, wq.shape[1]), BF16),
        ],
        scratch_shapes=[pltpu.VMEM((P, D), BF16)],
        compiler_params=pltpu.CompilerParams(
            dimension_semantics=("arbitrary", "arbitrary"), vmem_limit_bytes=VMEM_LIMIT),
        name="inproj",
    )(x, meta, g0, b0, wkv, bkv, wkbt, bkbt, wfbt, bfbt, wq, bq)


def _t5_bucket_table():
    qi = np.arange(BLOCK)[:, None] + BLOCK
    kj = np.arange(2 * BLOCK)[None, :]
    n = np.maximum(qi - kj, 0)
    max_exact = T5_BUCKETS // 2
    nf = np.maximum(n, 1).astype(np.float32)
    large = max_exact + (np.log(nf / np.float32(max_exact)) / np.float32(math.log(T5_MAX_DIST / max_exact))
                         * (T5_BUCKETS - max_exact)).astype(np.int32)
    large = np.minimum(large, T5_BUCKETS - 1)
    bucket = np.where(n < max_exact, n, large).astype(np.int32)
    dist = qi - kj
    window = (dist >= 0) & (dist < WINDOW)
    return np.where(window, bucket, -1).astype(np.int32)


def _swa_kernel(bucket_ref, rel_ref, sink_ref, q_ref, kvp_ref, kvc_ref, o_ref, bias_ref):
    b = pl.program_id(0)
    n = pl.program_id(1)

    @pl.when((b == 0) & (n == 0))
    def _():
        bucket = bucket_ref[...]
        for h in range(N_HEADS_A):
            acc = jnp.where(bucket < 0, NEG, 0.0).astype(F32)
            for t in range(T5_BUCKETS):
                acc = jnp.where(bucket == t, rel_ref[t, h], acc)
            bias_ref[h] = acc

    kcol = lax.broadcasted_iota(jnp.int32, (BLOCK, 2 * BLOCK), 1)
    kabs_ok = (n * BLOCK + kcol) >= PAD
    q = q_ref[0]
    kvp = kvp_ref[0]
    kvc = kvc_ref[0]
    outs = []
    for h in range(N_HEADS_A):
        g = h // GROUP_A
        k = jnp.concatenate([kvp[:, g * HEAD_DIM:(g + 1) * HEAD_DIM],
                             kvc[:, g * HEAD_DIM:(g + 1) * HEAD_DIM]], axis=0)
        v = jnp.concatenate([kvp[:, W_KA + g * HEAD_DIM:W_KA + (g + 1) * HEAD_DIM],
                             kvc[:, W_KA + g * HEAD_DIM:W_KA + (g + 1) * HEAD_DIM]], axis=0)
        s = _dot_nt(q[:, h * HEAD_DIM:(h + 1) * HEAD_DIM], k) + bias_ref[h]
        s = jnp.where(kabs_ok, s, NEG)
        sink = sink_ref[h]
        m = jnp.maximum(jnp.max(s, axis=-1, keepdims=True), sink)
        p = jnp.exp(s - m)
        denom = jnp.sum(p, axis=-1, keepdims=True) + jnp.exp(sink - m)
        o = _dot(p.astype(BF16), v)
        outs.append(o / denom)
    o_ref[0] = jnp.concatenate(outs, axis=-1).astype(BF16)


def _swa(zq, kva, rel_bias, sinks):
    B, S, _ = zq.shape
    nb = S // BLOCK
    bucket = jnp.asarray(_t5_bucket_table())
    smem = pl.BlockSpec(memory_space=pltpu.SMEM)
    return pl.pallas_call(
        _swa_kernel,
        grid=(B, nb),
        in_specs=[
            pl.BlockSpec((BLOCK, 2 * BLOCK), lambda b, n: (0, 0)),
            smem, smem,
            pl.BlockSpec((1, BLOCK, W_QA), lambda b, n: (b, n, 0)),
            pl.BlockSpec((1, BLOCK, 2 * W_KA), lambda b, n: (b, n, 0)),
            pl.BlockSpec((1, BLOCK, 2 * W_KA), lambda b, n: (b, n + 1, 0)),
        ],
        out_specs=pl.BlockSpec((1, BLOCK, W_QA), lambda b, n: (b, n, 0)),
        out_shape=jax.ShapeDtypeStruct((B, S, W_QA), BF16),
        scratch_shapes=[pltpu.VMEM((N_HEADS_A, BLOCK, 2 * BLOCK), F32)],
        compiler_params=pltpu.CompilerParams(
            dimension_semantics=("arbitrary", "arbitrary"), vmem_limit_bytes=VMEM_LIMIT),
        name="swa",
    )(bucket, rel_bias, sinks, zq, kva, kva)


FOX_TQ = 256
FOX_TK = 256


def _fox_kernel(q_ref, kt_ref, v_ref, c_ref, o_ref):
    pair = pl.program_id(1)
    qi = pl.program_id(2)
    tq = q_ref.shape[1]
    q0 = BLOCK + qi * tq
    q = q_ref[0]
    lane = lax.broadcasted_iota(jnp.int32, q.shape, 1)
    qpos = q0 + lax.broadcasted_iota(jnp.int32, (tq, FOX_TK), 0)
    kcol = lax.broadcasted_iota(jnp.int32, (tq, FOX_TK), 1)
    n_chunks = (q0 + tq + FOX_TK - 1) // FOX_TK
    res = []
    for i in range(2):
        qm = jnp.where((lane >= i * HEAD_DIM) & (lane < (i + 1) * HEAD_DIM), q, jnp.zeros_like(q))
        row = 2 * pair + i

        def crow_at(start, size, row=row):
            c8 = c_ref[0, :, pl.ds(start, size)]
            hsel = lax.broadcasted_iota(jnp.int32, c8.shape, 0) == row
            return jnp.sum(jnp.where(hsel, c8, 0.0), axis=0, keepdims=True)

        cq0 = crow_at(pl.multiple_of(q0, BLOCK), BLOCK)[:, 0:1]

        def body(jc, carry):
            m, l, acc = carry
            ks = pl.multiple_of(jc * FOX_TK, FOX_TK)
            kt = kt_ref[0, :, pl.ds(ks, FOX_TK)]
            vv = v_ref[0, pl.ds(ks, FOX_TK), :]
            crow = crow_at(ks, FOX_TK)
            s = _dot(qm, kt) + (cq0 - crow)
            kpos = ks + kcol
            valid = (kpos <= qpos) & (kpos >= PAD)
            s = jnp.where(valid, s, NEG)
            m_new = jnp.maximum(m, jnp.max(s, axis=-1, keepdims=True))
            alpha = jnp.exp(m - m_new)
            p = jnp.exp(s - m_new)
            l = alpha * l + jnp.sum(p, axis=-1, keepdims=True)
            acc = alpha * acc + _dot(p.astype(BF16), vv)
            return m_new, l, acc

        m0 = jnp.full((tq, 1), NEG, F32)
        l0 = jnp.zeros((tq, 1), F32)
        a0 = jnp.zeros((tq, 2 * HEAD_DIM), F32)
        m, l, acc = lax.fori_loop(0, n_chunks, body, (m0, l0, a0))
        res.append(acc / l)
    o_ref[0] = jnp.where(lane < HEAD_DIM, res[0], res[1]).astype(BF16)


def _fox(zq, kbt, vb, c):
    B, S, _ = zq.shape
    P = vb.shape[1]
    npair = N_HEADS_B // 2
    qblk0 = W_QA // (2 * HEAD_DIM)
    return pl.pallas_call(
        _fox_kernel,
        grid=(B, npair, S // FOX_TQ),
        in_specs=[
            pl.BlockSpec((1, FOX_TQ, 2 * HEAD_DIM), lambda b, p, i: (b, i, qblk0 + p)),
            pl.BlockSpec((1, 2 * HEAD_DIM, P), lambda b, p, i: (b, p, 0)),
            pl.BlockSpec((1, P, 2 * HEAD_DIM), lambda b, p, i: (b, 0, p)),
            pl.BlockSpec((1, N_HEADS_B, P), lambda b, p, i: (b, 0, 0)),
        ],
        out_specs=pl.BlockSpec((1, FOX_TQ, 2 * HEAD_DIM), lambda b, p, i: (b, i, p)),
        out_shape=jax.ShapeDtypeStruct((B, S, W_B), BF16),
        compiler_params=pltpu.CompilerParams(
            dimension_semantics=("arbitrary", "arbitrary", "arbitrary"), vmem_limit_bytes=VMEM_LIMIT),
        name="fox",
    )(zq, kbt, vb, c)


MERGE_TM = 512


def _merge_kernel(x_ref, oa_ref, ob_ref, ga_ref, gb_ref, g0_ref, b0_ref, wpa_ref, wpb_ref, wo_ref,
                  g1_ref, b1_ref, h1_ref, h1b_ref):
    h = _ln(x_ref[...], g0_ref[...], b0_ref[...])
    ya = _dot(oa_ref[...], wpa_ref[...])
    yb = _dot(ob_ref[...], wpb_ref[...])
    merged = jax.nn.sigmoid(ga_ref[...].astype(F32)) * ya + jax.nn.sigmoid(gb_ref[...].astype(F32)) * yb
    y = _dot(merged.astype(BF16), wo_ref[...])
    h1 = _ln(DEEPNORM_ALPHA * h + y, g1_ref[...], b1_ref[...])
    h1_ref[...] = h1
    h1b_ref[...] = h1.astype(BF16)


def _merge(x2, oa2, ob2, zq2, g0, b0, wpa, wpb, wo, g1, b1):
    T, D = x2.shape
    tm = MERGE_TM
    full = lambda a: pl.BlockSpec(a.shape, lambda i: (0,) * a.ndim)
    gate0 = (W_QA + W_B) // D
    return pl.pallas_call(
        _merge_kernel,
        grid=(T // tm,),
        in_specs=[
            pl.BlockSpec((tm, D), lambda i: (i, 0)),
            pl.BlockSpec((tm, W_QA), lambda i: (i, 0)),
            pl.BlockSpec((tm, W_B), lambda i: (i, 0)),
            pl.BlockSpec((tm, D), lambda i: (i, gate0)),
            pl.BlockSpec((tm, D), lambda i: (i, gate0 + 1)),
            full(g0), full(b0), full(wpa), full(wpb), full(wo), full(g1), full(b1),
        ],
        out_specs=[pl.BlockSpec((tm, D), lambda i: (i, 0)), pl.BlockSpec((tm, D), lambda i: (i, 0))],
        out_shape=[jax.ShapeDtypeStruct((T, D), F32), jax.ShapeDtypeStruct((T, D), BF16)],
        compiler_params=pltpu.CompilerParams(
            dimension_semantics=("arbitrary",), vmem_limit_bytes=VMEM_LIMIT),
        name="merge",
    )(x2, oa2, ob2, zq2, zq2, g0, b0, wpa, wpb, wo, g1, b1)


ROUTE_TM = 512
LANES = 128
HALF_DQ = PEER_DQ // 2
RANK_OUT = 64.0


def _top_sorted(s):
    rank = jnp.full(s.shape, RANK_OUT, F32)
    tops = []
    for r in range(PEER_TOPK):
        m = jnp.max(s, axis=0, keepdims=True)
        eq = s == m
        rank = jnp.where(eq, float(r), rank)
        s = jnp.where(eq, -jnp.inf, s)
        tops.append(m)
    return tops, rank


def _stack_rows(rows):
    n = len(rows)
    ri = lax.broadcasted_iota(jnp.int32, (n, rows[0].shape[1]), 0)
    out = jnp.broadcast_to(rows[0], ri.shape)
    for r in range(1, n):
        out = jnp.where(ri == r, rows[r], out)
    return out


def _route_kernel(h_ref, wqt_ref, sk_ref, na_ref, e1_ref, rb_ref, e2_ref, sc_ref):
    K = PEER_TOPK
    hb = h_ref[...]
    for hp in range(2 * PEER_HEADS):
        qt = _dot_nt(wqt_ref[hp * HALF_DQ:(hp + 1) * HALF_DQ, :], hb)
        sc_ref[hp] = _dot(sk_ref[hp], qt.astype(BF16))

    n_groups = h_ref.shape[0] // LANES

    def group(idx, _):
        h = idx // n_groups
        g = idx % n_groups
        lanes = pl.ds(pl.multiple_of(g * LANES, LANES), LANES)
        s1 = sc_ref[2 * h, :, lanes]
        s2 = sc_ref[2 * h + 1, :, lanes]
        a, ra = _top_sorted(s1)
        b, rb = _top_sorted(s2)
        bmat16 = _stack_rows(b)
        bmat8 = bmat16[:8]
        cands = [a[0] + bmat16]
        for i in range(1, 8):
            cands.append(a[i] + bmat8)
        cands.append(_stack_rows(a[8:]) + b[0])
        cand = jnp.concatenate(cands, axis=0)
        sel = jnp.zeros(cand.shape, F32)
        vmax = a[0] + b[0]
        z = jnp.zeros_like(vmax)
        for r in range(K):
            m = jnp.max(cand, axis=0, keepdims=True)
            eq = cand == m
            sel = jnp.where(eq, 1.0, sel)
            cand = jnp.where(eq, -jnp.inf, cand)
            z = z + jnp.exp(m - vmax)
        counts = [jnp.sum(sel[0:16], axis=0, keepdims=True)]
        for i in range(1, 8):
            counts.append(jnp.sum(sel[8 + 8 * i:16 + 8 * i], axis=0, keepdims=True))
        tail = sel[72:80]
        na = jnp.zeros(ra.shape, F32)
        for i in range(K):
            cnt = counts[i] if i < 8 else tail[i - 8:i - 7]
            na = jnp.where(ra == float(i), cnt, na)
        na_ref[h, :, lanes] = na
        e1_ref[h, :, lanes] = jnp.exp(s1 - a[0])
        rb_ref[h, :, lanes] = rb.astype(BF16)
        e2_ref[h, :, lanes] = (jnp.exp(s2 - b[0]) / z).astype(BF16)
        return 0

    lax.fori_loop(0, PEER_HEADS * n_groups, group, 0)


def _route(h1b, wqt, sk):
    T, D = h1b.shape
    tm = ROUTE_TM
    full = lambda a: pl.BlockSpec(a.shape, lambda i: (0,) * a.ndim)
    ospec = pl.BlockSpec((PEER_HEADS, PEER_NKEYS, tm), lambda i: (0, 0, i))
    oshape = lambda dt: jax.ShapeDtypeStruct((PEER_HEADS, PEER_NKEYS, T), dt)
    return pl.pallas_call(
        _route_kernel,
        grid=(T // tm,),
        in_specs=[pl.BlockSpec((tm, D), lambda i: (i, 0)), full(wqt), full(sk)],
        out_specs=[ospec, ospec, ospec, ospec],
        out_shape=[oshape(F32), oshape(F32), oshape(BF16), oshape(BF16)],
        scratch_shapes=[pltpu.VMEM((2 * PEER_HEADS, PEER_NKEYS, tm), F32)],
        compiler_params=pltpu.CompilerParams(
            dimension_semantics=("arbitrary",), vmem_limit_bytes=VMEM_LIMIT),
        name="route",
    )(h1b, wqt, sk)


EXP_TM = 512
EXP_TE = 512
SQRT_HALF = 0.7071067811865476


def _expert_kernel(xb_ref, h1_ref, u_ref, v_ref, na_ref, e1_ref, rb_ref, e2_ref, g2_ref, b2_ref,
                   o_ref, acc_ref, a_ref):
    j = pl.program_id(1)
    nj = pl.num_programs(1)
    tm = xb_ref.shape[0]
    n_ia = u_ref.shape[0] // PEER_NKEYS

    @pl.when(j == 0)
    def _():
        acc_ref[...] = jnp.zeros_like(acc_ref)

    xb = xb_ref[...]
    for il in range(n_ia):
        ia = j * n_ia + il
        hu = _dot_nt(u_ref[il * PEER_NKEYS:(il + 1) * PEER_NKEYS, :], xb)
        act = (0.5 * hu * (1.0 + lax.erf(hu * SQRT_HALF))).astype(BF16)
        gate = jnp.zeros((PEER_NKEYS, tm), BF16)
        for h in range(PEER_HEADS):
            na = na_ref[h, pl.ds(ia, 1), :].astype(BF16)
            e1 = e1_ref[h, pl.ds(ia, 1), :].astype(BF16)
            w = e2_ref[h] * e1
            gate = gate + jnp.where(rb_ref[h] < na, w, jnp.zeros_like(w))
        a_ref[il * PEER_NKEYS:(il + 1) * PEER_NKEYS, :] = act * gate
    acc_ref[...] += _dot_tn(a_ref[...], v_ref[...])

    @pl.when(j == nj - 1)
    def _():
        o_ref[...] = _ln(DEEPNORM_ALPHA * h1_ref[...] + acc_ref[...], g2_ref[...], b2_ref[...])


def _experts(h1b, h1, u, v, na, e1, rb, e2, g2, b2):
    T, D = h1.shape
    E = u.shape[0]
    tm, te = EXP_TM, EXP_TE
    gspec = pl.BlockSpec((PEER_HEADS, PEER_NKEYS, tm), lambda i, j: (0, 0, i))
    vec = pl.BlockSpec((1, D), lambda i, j: (0, 0))
    return pl.pallas_call(
        _expert_kernel,
        grid=(T // tm, E // te),
        in_specs=[
            pl.BlockSpec((tm, D), lambda i, j: (i, 0)),
            pl.BlockSpec((tm, D), lambda i, j: (i, 0)),
            pl.BlockSpec((te, D), lambda i, j: (j, 0)),
            pl.BlockSpec((te, D), lambda i, j: (j, 0)),
            gspec, gspec, gspec, gspec, vec, vec,
        ],
        out_specs=pl.BlockSpec((tm, D), lambda i, j: (i, 0)),
        out_shape=jax.ShapeDtypeStruct((T, D), F32),
        scratch_shapes=[pltpu.VMEM((tm, D), F32), pltpu.VMEM((te, tm), BF16)],
        compiler_params=pltpu.CompilerParams(
            dimension_semantics=("arbitrary", "arbitrary"), vmem_limit_bytes=VMEM_LIMIT),
        name="experts",
    )(h1b, h1, u, v, na, e1, rb, e2, g2, b2)


def kernel(x, meta_tokens, emb_ln_g, emb_ln_b, rel_bias, w_in, b_in, attn_sinks, w_pa, w_pb, w_o,
           ln1_g, ln1_b, peer_wq, peer_subkeys, peer_u, peer_v, ln2_g, ln2_b):
    B, S, D = x.shape
    T = B * S
    row = lambda a: a.reshape(1, -1).astype(F32)

    w, bi = w_in[0], b_in[0]
    sizes = (W_QA, W_KA, W_KA, W_B, W_B, W_B, N_HEADS_B, D, D)
    offs = np.cumsum((0,) + sizes)
    col = lambda a, k: a[..., offs[k]:offs[k + 1]]
    scale = HEAD_DIM ** -0.5
    wq = jnp.concatenate([col(w, 0) * scale, col(w, 3) * scale, col(w, 7), col(w, 8)], axis=-1).astype(BF16)
    bq = jnp.concatenate([col(bi, 0) * scale, col(bi, 3) * scale, col(bi, 7), col(bi, 8)], axis=-1)
    wkv = jnp.concatenate([col(w, 1), col(w, 2), col(w, 5)], axis=-1).astype(BF16)
    bkv = jnp.concatenate([col(bi, 1), col(bi, 2), col(bi, 5)], axis=-1)
    wkbt = col(w, 4).T.astype(BF16)
    bkbt = col(bi, 4).reshape(-1, 1).astype(F32)
    fpad = 16 - N_HEADS_B
    wfbt = jnp.pad(col(w, 6).T, ((0, fpad), (0, 0))).astype(BF16)
    bfbt = jnp.pad(col(bi, 6), (0, fpad)).reshape(-1, 1).astype(F32)

    g0, b0 = row(emb_ln_g), row(emb_ln_b)
    kva, vb, kbt, c, zq = _inproj(x, meta_tokens.astype(F32), g0, b0, wkv, row(bkv), wkbt, bkbt,
                                  wfbt, bfbt, wq, row(bq))
    oa = _swa(zq, kva, rel_bias.astype(F32), attn_sinks[0].astype(F32))
    ob = _fox(zq, kbt, vb, c)

    h1, h1b = _merge(x.reshape(T, D), oa.reshape(T, W_QA), ob.reshape(T, W_B), zq.reshape(T, -1),
                     g0, b0, w_pa[0].astype(BF16), w_pb[0].astype(BF16), w_o[0].astype(BF16),
                     row(ln1_g[0]), row(ln1_b[0]))

    wqt = peer_wq[0].T.astype(BF16)
    sk = peer_subkeys[0].reshape(2 * PEER_HEADS, PEER_NKEYS, HALF_DQ).astype(BF16)
    na, e1, rb, e2 = _route(h1b, wqt, sk)
    out = _experts(h1b, h1, peer_u[0].astype(BF16), peer_v[0].astype(BF16), na, e1, rb, e2,
                   row(ln2_g[0]), row(ln2_b[0]))
    return out.reshape(B, S, D)
```

```python
import functools
import math

import jax
import jax.numpy as jnp
import numpy as np
from jax import lax
from jax.experimental import pallas as pl
from jax.experimental.pallas import tpu as pltpu

N_META = 16
BLOCK = 128
PAD = BLOCK - N_META
HEAD_DIM = 64
N_HEADS_A = 8
N_KV_A = 2
GROUP_A = N_HEADS_A // N_KV_A
WINDOW = 128
N_HEADS_B = 8
T5_BUCKETS = 32
T5_MAX_DIST = 128
PEER_HEADS = 8
PEER_NKEYS = 128
PEER_DQ = 256
PEER_TOPK = 16
LN_EPS = 1e-5
NEG = -1e30
DEPTH = 1
DEEPNORM_ALPHA = (2 * DEPTH) ** 0.25

W_QA = N_HEADS_A * HEAD_DIM
W_KA = N_KV_A * HEAD_DIM
W_B = N_HEADS_B * HEAD_DIM

F32 = jnp.float32
BF16 = jnp.bfloat16

VMEM_LIMIT = 56 * 1024 * 1024


def _ln(x, g, b):
    mu = jnp.mean(x, axis=-1, keepdims=True)
    xc = x - mu
    var = jnp.mean(xc * xc, axis=-1, keepdims=True)
    return xc * lax.rsqrt(var + LN_EPS) * g + b


def _dot(a, b):
    return jnp.dot(a, b, preferred_element_type=F32)


def _dot_nt(a, b):
    return lax.dot_general(a, b, (((1,), (1,)), ((), ())), preferred_element_type=F32)


def _dot_tn(a, b):
    return lax.dot_general(a, b, (((0,), (0,)), ((), ())), preferred_element_type=F32)


def _split3(x):
    hi = x.astype(BF16)
    r = x - hi.astype(F32)
    mid = r.astype(BF16)
    lo = (r - mid.astype(F32)).astype(BF16)
    return hi, mid, lo


QT_COLS = 512


def _inproj_kernel(x_ref, meta_ref, g0_ref, b0_ref, wkv_ref, bkv_ref, wkbt_ref, bkbt_ref,
                   wfbt_ref, bfbt_ref, wq_ref, bq_ref,
                   kva_ref, vb_ref, kbt_ref, c_ref, zq_ref, hs_ref):
    j = pl.program_id(1)
    P = hs_ref.shape[0]

    @pl.when(j == 0)
    def _():
        g0 = g0_ref[...]
        b0 = b0_ref[...]
        hs_ref[0:PAD, :] = jnp.zeros((PAD, hs_ref.shape[1]), BF16)
        hs_ref[PAD:BLOCK, :] = _ln(meta_ref[...], g0, b0).astype(BF16)
        hs_ref[BLOCK:, :] = _ln(x_ref[0], g0, b0).astype(BF16)
        hs = hs_ref[...]
        nkv = 2 * W_KA
        kva_ref[0] = (_dot(hs, wkv_ref[:, :nkv]) + bkv_ref[:, :nkv]).astype(BF16)
        vb_ref[0, :P, :] = (_dot(hs, wkv_ref[:, nkv:]) + bkv_ref[:, nkv:]).astype(BF16)
        vb_ref[0, P:, :] = jnp.zeros((vb_ref.shape[1] - P, vb_ref.shape[2]), BF16)
        kbt_ref[0, :, :P] = (_dot_nt(wkbt_ref[...], hs) + bkbt_ref[...]).astype(BF16)
        kbt_ref[0, :, P:] = jnp.zeros((kbt_ref.shape[1], kbt_ref.shape[2] - P), BF16)
        c_ref[0, :, P:] = jnp.full((c_ref.shape[1], c_ref.shape[2] - P), C_MASKED, F32)
        fbt = _dot_nt(wfbt_ref[...], hs) + bfbt_ref[...]
        lf = jax.nn.log_sigmoid(fbt)
        pos = lax.broadcasted_iota(jnp.int32, lf.shape, 1)
        lf = jnp.where(pos >= PAD, lf, 0.0)
        ri = lax.broadcasted_iota(jnp.int32, (BLOCK, BLOCK), 0)
        ci = lax.broadcasted_iota(jnp.int32, (BLOCK, BLOCK), 1)
        tri = jnp.where(ri <= ci, 1.0, 0.0).astype(BF16)
        carry = jnp.zeros((lf.shape[0], 1), F32)
        for n in range(P // BLOCK):
            hi, mid, lo = _split3(lf[:, n * BLOCK:(n + 1) * BLOCK])
            cb = _dot(hi, tri) + _dot(mid, tri) + _dot(lo, tri) + carry
            carry = cb[:, BLOCK - 1:BLOCK]
            if n == 0:
                cb = jnp.where(lax.broadcasted_iota(jnp.int32, cb.shape, 1) >= PAD, cb, C_MASKED)
            c_ref[0, :, n * BLOCK:(n + 1) * BLOCK] = cb[:N_HEADS_B]

    @pl.when(j > 0)
    def _():
        zq_ref[0] = (_dot(hs_ref[BLOCK:, :], wq_ref[...]) + bq_ref[...]).astype(BF16)


def _inproj(x, meta, g0, b0, wkv, bkv, wkbt, bkbt, wfbt, bfbt, wq, bq):
    B, S, D = x.shape
    P = S + BLOCK
    P2 = -(-P // FOX_TK) * FOX_TK
    nq = wq.shape[1] // QT_COLS
    full = lambda shape: pl.BlockSpec(shape, lambda b, j: (0,) * len(shape))
    qcol = lambda b, j: (0, jnp.maximum(j - 1, 0))
    return pl.pallas_call(
        _inproj_kernel,
        grid=(B, nq + 1),
        in_specs=[
            pl.BlockSpec((1, S, D), lambda b, j: (b, 0, 0)),
            full(meta.shape), full(g0.shape), full(b0.shape),
            full(wkv.shape), full(bkv.shape), full(wkbt.shape), full(bkbt.shape),
            full(wfbt.shape), full(bfbt.shape),
            pl.BlockSpec((D, QT_COLS), qcol),
            pl.BlockSpec((1, QT_COLS), qcol),
        ],
        out_specs=[
            pl.BlockSpec((1, P, 2 * W_KA), lambda b, j: (b, 0, 0)),
            pl.BlockSpec((1, P2, W_B), lambda b, j: (b, 0, 0)),
            pl.BlockSpec((1, W_B, P2), lambda b, j: (b, 0, 0)),
            pl.BlockSpec((1, N_HEADS_B, P2), lambda b, j: (b, 0, 0)),
            pl.BlockSpec((1, S, QT_COLS), lambda b, j: (b, 0, jnp.maximum(j - 1, 0))),
        ],
        out_shape=[
            jax.ShapeDtypeStruct((B, P, 2 * W_KA), BF16),
            jax.ShapeDtypeStruct((B, P2, W_B), BF16),
            jax.ShapeDtypeStruct((B, W_B, P2), BF16),
            jax.ShapeDtypeStruct((B, N_HEADS_B, P2), F32),
            jax.ShapeDtypeStruct((B, S, wq.shape[1]), BF16),
        ],
        scratch_shapes=[pltpu.VMEM((P, D), BF16)],
        compiler_params=pltpu.CompilerParams(
            dimension_semantics=("arbitrary", "arbitrary"), vmem_limit_bytes=VMEM_LIMIT),
        name="inproj",
    )(x, meta, g0, b0, wkv, bkv, wkbt, bkbt, wfbt, bfbt, wq, bq)


def _t5_bucket_table():
    qi = np.arange(BLOCK)[:, None] + BLOCK
    kj = np.arange(2 * BLOCK)[None, :]
    n = np.maximum(qi - kj, 0)
    max_exact = T5_BUCKETS // 2
    nf = np.maximum(n, 1).astype(np.float32)
    large = max_exact + (np.log(nf / np.float32(max_exact)) / np.float32(math.log(T5_MAX_DIST / max_exact))
                         * (T5_BUCKETS - max_exact)).astype(np.int32)
    large = np.minimum(large, T5_BUCKETS - 1)
    bucket = np.where(n < max_exact, n, large).astype(np.int32)
    dist = qi - kj
    window = (dist >= 0) & (dist < WINDOW)
    return np.where(window, bucket, -1).astype(np.int32)


def _swa_kernel(bucket_ref, rel_ref, sink_ref, q_ref, kvp_ref, kvc_ref, o_ref, bias_ref):
    b = pl.program_id(0)
    n = pl.program_id(1)

    @pl.when((b == 0) & (n == 0))
    def _():
        bucket = bucket_ref[...]
        for h in range(N_HEADS_A):
            acc = jnp.where(bucket < 0, NEG, 0.0).astype(F32)
            for t in range(T5_BUCKETS):
                acc = jnp.where(bucket == t, rel_ref[t, h], acc)
            bias_ref[h] = acc

    kcol = lax.broadcasted_iota(jnp.int32, (BLOCK, 2 * BLOCK), 1)
    kabs_ok = (n * BLOCK + kcol) >= PAD
    q = q_ref[0]
    kvp = kvp_ref[0]
    kvc = kvc_ref[0]
    outs = []
    for h in range(N_HEADS_A):
        g = h // GROUP_A
        k = jnp.concatenate([kvp[:, g * HEAD_DIM:(g + 1) * HEAD_DIM],
                             kvc[:, g * HEAD_DIM:(g + 1) * HEAD_DIM]], axis=0)
        v = jnp.concatenate([kvp[:, W_KA + g * HEAD_DIM:W_KA + (g + 1) * HEAD_DIM],
                             kvc[:, W_KA + g * HEAD_DIM:W_KA + (g + 1) * HEAD_DIM]], axis=0)
        s = _dot_nt(q[:, h * HEAD_DIM:(h + 1) * HEAD_DIM], k) + bias_ref[h]
        s = jnp.where(kabs_ok, s, NEG)
        sink = sink_ref[h]
        m = jnp.maximum(jnp.max(s, axis=-1, keepdims=True), sink)
        p = jnp.exp(s - m)
        denom = jnp.sum(p, axis=-1, keepdims=True) + jnp.exp(sink - m)
        o = _dot(p.astype(BF16), v)
        outs.append(o / denom)
    o_ref[0] = jnp.concatenate(outs, axis=-1).astype(BF16)


def _swa(zq, kva, rel_bias, sinks):
    B, S, _ = zq.shape
    nb = S // BLOCK
    bucket = jnp.asarray(_t5_bucket_table())
    smem = pl.BlockSpec(memory_space=pltpu.SMEM)
    return pl.pallas_call(
        _swa_kernel,
        grid=(B, nb),
        in_specs=[
            pl.BlockSpec((BLOCK, 2 * BLOCK), lambda b, n: (0, 0)),
            smem, smem,
            pl.BlockSpec((1, BLOCK, W_QA), lambda b, n: (b, n, 0)),
            pl.BlockSpec((1, BLOCK, 2 * W_KA), lambda b, n: (b, n, 0)),
            pl.BlockSpec((1, BLOCK, 2 * W_KA), lambda b, n: (b, n + 1, 0)),
        ],
        out_specs=pl.BlockSpec((1, BLOCK, W_QA), lambda b, n: (b, n, 0)),
        out_shape=jax.ShapeDtypeStruct((B, S, W_QA), BF16),
        scratch_shapes=[pltpu.VMEM((N_HEADS_A, BLOCK, 2 * BLOCK), F32)],
        compiler_params=pltpu.CompilerParams(
            dimension_semantics=("arbitrary", "arbitrary"), vmem_limit_bytes=VMEM_LIMIT),
        name="swa",
    )(bucket, rel_bias, sinks, zq, kva, kva)


FOX_TQ = 256
FOX_TK = 512
C_MASKED = 1e30


def _fox_kernel(q_ref, kt_ref, v_ref, c_ref, o_ref):
    pair = pl.program_id(1)
    qi = pl.program_id(2)
    tq = q_ref.shape[1]
    tk = FOX_TK
    q0 = BLOCK + qi * tq
    q = q_ref[0]
    first = lax.broadcasted_iota(jnp.int32, q.shape, 1) < HEAD_DIM
    qzero = jnp.zeros_like(q)
    qm = (jnp.where(first, q, qzero), jnp.where(first, qzero, q))
    qpos = q0 + lax.broadcasted_iota(jnp.int32, (tq, tk), 0)
    kcol = lax.broadcasted_iota(jnp.int32, (tq, tk), 1)

    def crows(start, size):
        c8 = c_ref[0, :, pl.ds(start, size)]
        hrow = lax.broadcasted_iota(jnp.int32, c8.shape, 0)
        return tuple(jnp.sum(jnp.where(hrow == 2 * pair + i, c8, 0.0), axis=0, keepdims=True)
                     for i in range(2))

    cq0 = tuple(r[:, 0:1] for r in crows(pl.multiple_of(q0, BLOCK), BLOCK))

    def chunk(jc, carry, causal):
        m, l, acc = carry
        ks = pl.multiple_of(jc * tk, tk)
        kt = kt_ref[0, :, pl.ds(ks, tk)]
        vv = v_ref[0, pl.ds(ks, tk), :]
        vfirst = lax.broadcasted_iota(jnp.int32, vv.shape, 1) < HEAD_DIM
        vzero = jnp.zeros_like(vv)
        vm = (jnp.where(vfirst, vv, vzero), jnp.where(vfirst, vzero, vv))
        cr = crows(ks, tk)
        m_out, l_out, alphas, pv = [], [], [], None
        for i in range(2):
            s = _dot(qm[i], kt) + (cq0[i] - cr[i])
            if causal:
                s = jnp.where(ks + kcol <= qpos, s, NEG)
            m_new = jnp.maximum(m[i], jnp.max(s, axis=-1, keepdims=True))
            alpha = jnp.exp(m[i] - m_new)
            p = jnp.exp(s - m_new)
            m_out.append(m_new)
            l_out.append(alpha * l[i] + jnp.sum(p, axis=-1, keepdims=True))
            alphas.append(alpha)
            d = _dot(p.astype(BF16), vm[i])
            pv = d if pv is None else pv + d
        acc = jnp.where(first, alphas[0], alphas[1]) * acc + pv
        return tuple(m_out), tuple(l_out), acc

    n_full = q0 // tk
    n_chunks = (q0 + tq + tk - 1) // tk
    neg = jnp.full((tq, 1), NEG, F32)
    zero = jnp.zeros((tq, 1), F32)
    carry = ((neg, neg), (zero, zero), jnp.zeros((tq, 2 * HEAD_DIM), F32))
    carry = lax.fori_loop(0, n_full, functools.partial(chunk, causal=False), carry)
    m, l, acc = lax.fori_loop(n_full, n_chunks, functools.partial(chunk, causal=True), carry)
    o_ref[0] = (acc / jnp.where(first, l[0], l[1])).astype(BF16)


def _fox(zq, kbt, vb, c):
    B, S, _ = zq.shape
    P = vb.shape[1]
    npair = N_HEADS_B // 2
    qblk0 = W_QA // (2 * HEAD_DIM)
    return pl.pallas_call(
        _fox_kernel,
        grid=(B, npair, S // FOX_TQ),
        in_specs=[
            pl.BlockSpec((1, FOX_TQ, 2 * HEAD_DIM), lambda b, p, i: (b, i, qblk0 + p)),
            pl.BlockSpec((1, 2 * HEAD_DIM, P), lambda b, p, i: (b, p, 0)),
            pl.BlockSpec((1, P, 2 * HEAD_DIM), lambda b, p, i: (b, 0, p)),
            pl.BlockSpec((1, N_HEADS_B, P), lambda b, p, i: (b, 0, 0)),
        ],
        out_specs=pl.BlockSpec((1, FOX_TQ, 2 * HEAD_DIM), lambda b, p, i: (b, i, p)),
        out_shape=jax.ShapeDtypeStruct((B, S, W_B), BF16),
        compiler_params=pltpu.CompilerParams(
            dimension_semantics=("arbitrary", "arbitrary", "arbitrary"), vmem_limit_bytes=VMEM_LIMIT),
        name="fox",
    )(zq, kbt, vb, c)


MERGE_TM = 512


def _merge_kernel(x_ref, oa_ref, ob_ref, ga_ref, gb_ref, g0_ref, b0_ref, wpa_ref, wpb_ref, wo_ref,
                  g1_ref, b1_ref, h1_ref, h1t_ref):
    h = _ln(x_ref[...], g0_ref[...], b0_ref[...])
    ya = _dot(oa_ref[...], wpa_ref[...])
    yb = _dot(ob_ref[...], wpb_ref[...])
    merged = jax.nn.sigmoid(ga_ref[...].astype(F32)) * ya + jax.nn.sigmoid(gb_ref[...].astype(F32)) * yb
    y = _dot(merged.astype(BF16), wo_ref[...])
    h1 = _ln(DEEPNORM_ALPHA * h + y, g1_ref[...], b1_ref[...])
    h1_ref[...] = h1
    h1t_ref[...] = h1.T.astype(BF16)


def _merge(x2, oa2, ob2, zq2, g0, b0, wpa, wpb, wo, g1, b1):
    T, D = x2.shape
    tm = MERGE_TM
    full = lambda a: pl.BlockSpec(a.shape, lambda i: (0,) * a.ndim)
    gate0 = (W_QA + W_B) // D
    return pl.pallas_call(
        _merge_kernel,
        grid=(T // tm,),
        in_specs=[
            pl.BlockSpec((tm, D), lambda i: (i, 0)),
            pl.BlockSpec((tm, W_QA), lambda i: (i, 0)),
            pl.BlockSpec((tm, W_B), lambda i: (i, 0)),
            pl.BlockSpec((tm, D), lambda i: (i, gate0)),
            pl.BlockSpec((tm, D), lambda i: (i, gate0 + 1)),
            full(g0), full(b0), full(wpa), full(wpb), full(wo), full(g1), full(b1),
        ],
        out_specs=[pl.BlockSpec((tm, D), lambda i: (i, 0)), pl.BlockSpec((D, tm), lambda i: (0, i))],
        out_shape=[jax.ShapeDtypeStruct((T, D), F32), jax.ShapeDtypeStruct((D, T), BF16)],
        compiler_params=pltpu.CompilerParams(
            dimension_semantics=("arbitrary",), vmem_limit_bytes=VMEM_LIMIT),
        name="merge",
    )(x2, oa2, ob2, zq2, zq2, g0, b0, wpa, wpb, wo, g1, b1)


ROUTE_TM = 512
LANES = 128
HALF_DQ = PEER_DQ // 2
RANK_OUT = 64.0


def _top_sorted(s):
    rank = jnp.full(s.shape, RANK_OUT, F32)
    tops = []
    for r in range(PEER_TOPK):
        m = jnp.max(s, axis=0, keepdims=True)
        eq = s == m
        rank = jnp.where(eq, float(r), rank)
        s = jnp.where(eq, -jnp.inf, s)
        tops.append(m)
    return tops, rank


def _stack_rows(rows):
    n = len(rows)
    ri = lax.broadcasted_iota(jnp.int32, (n, rows[0].shape[1]), 0)
    out = jnp.broadcast_to(rows[0], ri.shape)
    for r in range(1, n):
        out = jnp.where(ri == r, rows[r], out)
    return out


def _route_kernel(ht_ref, wqt_ref, sk_ref, na_ref, e1_ref, rb_ref, e2_ref, q_ref, sc_ref):
    K = PEER_TOPK
    q_ref[...] = _dot(wqt_ref[...], ht_ref[...]).astype(BF16)
    for h in range(PEER_HEADS):
        sc_ref[h] = _dot(sk_ref[h], q_ref[h * PEER_DQ:(h + 1) * PEER_DQ, :])

    n_groups = ht_ref.shape[1] // LANES

    def group(idx, _):
        h = idx // n_groups
        g = idx % n_groups
        lanes = pl.ds(pl.multiple_of(g * LANES, LANES), LANES)
        s1 = sc_ref[h, :PEER_NKEYS, lanes]
        s2 = sc_ref[h, PEER_NKEYS:, lanes]
        a, ra = _top_sorted(s1)
        b, rb = _top_sorted(s2)
        bmat16 = _stack_rows(b)
        bmat8 = bmat16[:8]
        cands = [a[0] + bmat16]
        for i in range(1, 8):
            cands.append(a[i] + bmat8)
        cands.append(_stack_rows(a[8:]) + b[0])
        cand = jnp.concatenate(cands, axis=0)
        vmax = a[0] + b[0]
        z = jnp.zeros_like(vmax)
        for r in range(K):
            m = jnp.max(cand, axis=0, keepdims=True)
            cand = jnp.where(cand == m, -jnp.inf, cand)
            z = z + jnp.exp(m - vmax)
        sel = jnp.where(cand == -jnp.inf, 1.0, 0.0)
        counts = [jnp.sum(sel[0:16], axis=0, keepdims=True)]
        for i in range(1, 8):
            counts.append(jnp.sum(sel[8 + 8 * i:16 + 8 * i], axis=0, keepdims=True))
        n_tail = jnp.sum(sel[72:80], axis=0, keepdims=True)
        na = jnp.where(ra < 8.0 + n_tail, 1.0, 0.0)
        for i in range(8):
            na = jnp.where(ra == float(i), counts[i], na)
        na_ref[h, :, lanes] = na
        e1_ref[h, :, lanes] = jnp.exp(s1 - a[0])
        rb_ref[h, :, lanes] = rb.astype(BF16)
        e2_ref[h, :, lanes] = (jnp.exp(s2 - b[0]) * (0.5 / z)).astype(BF16)
        return 0

    lax.fori_loop(0, PEER_HEADS * n_groups, group, 0)


def _route(h1t, wqt, skbd):
    D, T = h1t.shape
    tm = ROUTE_TM
    full = lambda a: pl.BlockSpec(a.shape, lambda i: (0,) * a.ndim)
    ospec = pl.BlockSpec((PEER_HEADS, PEER_NKEYS, tm), lambda i: (0, 0, i))
    oshape = lambda dt: jax.ShapeDtypeStruct((PEER_HEADS, PEER_NKEYS, T), dt)
    return pl.pallas_call(
        _route_kernel,
        grid=(T // tm,),
        in_specs=[pl.BlockSpec((D, tm), lambda i: (0, i)), full(wqt), full(skbd)],
        out_specs=[ospec, ospec, ospec, ospec],
        out_shape=[oshape(F32), oshape(F32), oshape(BF16), oshape(BF16)],
        scratch_shapes=[pltpu.VMEM((PEER_HEADS * PEER_DQ, tm), BF16),
                        pltpu.VMEM((PEER_HEADS, 2 * PEER_NKEYS, tm), F32)],
        compiler_params=pltpu.CompilerParams(
            dimension_semantics=("arbitrary",), vmem_limit_bytes=VMEM_LIMIT),
        name="route",
    )(h1t, wqt, skbd)


EXP_TM = 512
EXP_TE = 1024
EXP_IA = EXP_TE // PEER_NKEYS
BF16_ROWS = 16
SQRT_HALF = 0.7071067811865476


def _expert_kernel(xt_ref, h1_ref, u_ref, v_ref, na_ref, e1_ref, rb_ref, e2_ref, g2_ref, b2_ref,
                   o_ref, acc_ref, hu_ref, a_ref):
    j = pl.program_id(1)
    nj = pl.num_programs(1)
    tm = xt_ref.shape[1]
    nsub = PEER_NKEYS // BF16_ROWS

    @pl.when(j == 0)
    def _():
        acc_ref[...] = jnp.zeros_like(acc_ref)

    hu_ref[...] = _dot(u_ref[...], xt_ref[...])
    for il in range(EXP_IA):
        row = pl.ds(j * EXP_IA + il, 1)
        sl = slice(il * PEER_NKEYS, (il + 1) * PEER_NKEYS)
        hu = hu_ref[sl, :]
        act = (hu * (1.0 + lax.erf(hu * SQRT_HALF))).astype(BF16)
        gate = jnp.zeros((nsub, BF16_ROWS, tm), BF16)
        for h in range(PEER_HEADS):
            na = jnp.broadcast_to(na_ref[h, row, :], (BF16_ROWS, tm)).astype(BF16)
            e1 = jnp.broadcast_to(e1_ref[h, row, :], (BF16_ROWS, tm)).astype(BF16)
            rb = rb_ref[h].reshape(nsub, BF16_ROWS, tm)
            w = e2_ref[h].reshape(nsub, BF16_ROWS, tm) * e1[None]
            gate = gate + jnp.where(rb < na[None], w, jnp.zeros_like(w))
        a_ref[sl, :] = act * gate.reshape(PEER_NKEYS, tm)
    acc_ref[...] += _dot_tn(a_ref[...], v_ref[...])

    @pl.when(j == nj - 1)
    def _():
        o_ref[...] = _ln(DEEPNORM_ALPHA * h1_ref[...] + acc_ref[...], g2_ref[...], b2_ref[...])


def _experts(h1t, h1, u, v, na, e1, rb, e2, g2, b2):
    T, D = h1.shape
    E = u.shape[0]
    tm, te = EXP_TM, EXP_TE
    gspec = pl.BlockSpec((PEER_HEADS, PEER_NKEYS, tm), lambda i, j: (0, 0, i))
    vec = pl.BlockSpec((1, D), lambda i, j: (0, 0))
    return pl.pallas_call(
        _expert_kernel,
        grid=(T // tm, E // te),
        in_specs=[
            pl.BlockSpec((D, tm), lambda i, j: (0, i)),
            pl.BlockSpec((tm, D), lambda i, j: (i, 0)),
            pl.BlockSpec((te, D), lambda i, j: (j, 0)),
            pl.BlockSpec((te, D), lambda i, j: (j, 0)),
            gspec, gspec, gspec, gspec, vec, vec,
        ],
        out_specs=pl.BlockSpec((tm, D), lambda i, j: (i, 0)),
        out_shape=jax.ShapeDtypeStruct((T, D), F32),
        scratch_shapes=[pltpu.VMEM((tm, D), F32), pltpu.VMEM((te, tm), F32), pltpu.VMEM((te, tm), BF16)],
        compiler_params=pltpu.CompilerParams(
            dimension_semantics=("arbitrary", "arbitrary"), vmem_limit_bytes=VMEM_LIMIT),
        name="experts",
    )(h1t, h1, u, v, na, e1, rb, e2, g2, b2)


def kernel(x, meta_tokens, emb_ln_g, emb_ln_b, rel_bias, w_in, b_in, attn_sinks, w_pa, w_pb, w_o,
           ln1_g, ln1_b, peer_wq, peer_subkeys, peer_u, peer_v, ln2_g, ln2_b):
    B, S, D = x.shape
    T = B * S
    row = lambda a: a.reshape(1, -1).astype(F32)

    w, bi = w_in[0], b_in[0]
    sizes = (W_QA, W_KA, W_KA, W_B, W_B, W_B, N_HEADS_B, D, D)
    offs = np.cumsum((0,) + sizes)
    col = lambda a, k: a[..., offs[k]:offs[k + 1]]
    scale = HEAD_DIM ** -0.5
    wq = jnp.concatenate([col(w, 0) * scale, col(w, 3) * scale, col(w, 7), col(w, 8)], axis=-1).astype(BF16)
    bq = jnp.concatenate([col(bi, 0) * scale, col(bi, 3) * scale, col(bi, 7), col(bi, 8)], axis=-1)
    wkv = jnp.concatenate([col(w, 1), col(w, 2), col(w, 5)], axis=-1).astype(BF16)
    bkv = jnp.concatenate([col(bi, 1), col(bi, 2), col(bi, 5)], axis=-1)
    wkbt = col(w, 4).T.astype(BF16)
    bkbt = col(bi, 4).reshape(-1, 1).astype(F32)
    fpad = 16 - N_HEADS_B
    wfbt = jnp.pad(col(w, 6).T, ((0, fpad), (0, 0))).astype(BF16)
    bfbt = jnp.pad(col(bi, 6), (0, fpad)).reshape(-1, 1).astype(F32)

    g0, b0 = row(emb_ln_g), row(emb_ln_b)
    kva, vb, kbt, c, zq = _inproj(x, meta_tokens.astype(F32), g0, b0, wkv, row(bkv), wkbt, bkbt,
                                  wfbt, bfbt, wq, row(bq))
    oa = _swa(zq, kva, rel_bias.astype(F32), attn_sinks[0].astype(F32))
    ob = _fox(zq, kbt, vb, c)

    h1, h1t = _merge(x.reshape(T, D), oa.reshape(T, W_QA), ob.reshape(T, W_B), zq.reshape(T, -1),
                     g0, b0, w_pa[0].astype(BF16), w_pb[0].astype(BF16), w_o[0].astype(BF16),
                     row(ln1_g[0]), row(ln1_b[0]))

    wqt = peer_wq[0].T.astype(BF16)
    sk = peer_subkeys[0].astype(BF16)
    zero = jnp.zeros_like(sk[:, 0])
    skbd = jnp.concatenate([jnp.concatenate([sk[:, 0], zero], axis=-1),
                            jnp.concatenate([zero, sk[:, 1]], axis=-1)], axis=1)
    na, e1, rb, e2 = _route(h1t, wqt, skbd)
    out = _experts(h1t, h1, peer_u[0].astype(BF16), peer_v[0].astype(BF16), na, e1, rb, e2,
                   row(ln2_g[0]), row(ln2_b[0]))
    return out.reshape(B, S, D)
```

```python
import functools
import math

import jax
import jax.numpy as jnp
import numpy as np
from jax import lax
from jax.experimental import pallas as pl
from jax.experimental.pallas import tpu as pltpu

N_META = 16
BLOCK = 128
PAD = BLOCK - N_META
HEAD_DIM = 64
N_HEADS_A = 8
N_KV_A = 2
GROUP_A = N_HEADS_A // N_KV_A
WINDOW = 128
N_HEADS_B = 8
T5_BUCKETS = 32
T5_MAX_DIST = 128
PEER_HEADS = 8
PEER_NKEYS = 128
PEER_DQ = 256
PEER_TOPK = 16
LN_EPS = 1e-5
NEG = -1e30
DEPTH = 1
DEEPNORM_ALPHA = (2 * DEPTH) ** 0.25

W_QA = N_HEADS_A * HEAD_DIM
W_KA = N_KV_A * HEAD_DIM
W_B = N_HEADS_B * HEAD_DIM

F32 = jnp.float32
BF16 = jnp.bfloat16

VMEM_LIMIT = 56 * 1024 * 1024


def _ln(x, g, b):
    mu = jnp.mean(x, axis=-1, keepdims=True)
    xc = x - mu
    var = jnp.mean(xc * xc, axis=-1, keepdims=True)
    return xc * lax.rsqrt(var + LN_EPS) * g + b


def _dot(a, b):
    return jnp.dot(a, b, preferred_element_type=F32)


def _dot_nt(a, b):
    return lax.dot_general(a, b, (((1,), (1,)), ((), ())), preferred_element_type=F32)


def _dot_tn(a, b):
    return lax.dot_general(a, b, (((0,), (0,)), ((), ())), preferred_element_type=F32)


def _split3(x):
    hi = x.astype(BF16)
    r = x - hi.astype(F32)
    mid = r.astype(BF16)
    lo = (r - mid.astype(F32)).astype(BF16)
    return hi, mid, lo


QT_COLS = 512


def _inproj_kernel(x_ref, meta_ref, g0_ref, b0_ref, wkv_ref, bkv_ref, wkbt_ref, bkbt_ref,
                   wfbt_ref, bfbt_ref, wq_ref, bq_ref,
                   kva_ref, vb_ref, kbt_ref, c_ref, zq_ref, hs_ref):
    j = pl.program_id(1)
    P = hs_ref.shape[0]

    @pl.when(j == 0)
    def _():
        g0 = g0_ref[...]
        b0 = b0_ref[...]
        hs_ref[0:PAD, :] = jnp.zeros((PAD, hs_ref.shape[1]), BF16)
        hs_ref[PAD:BLOCK, :] = _ln(meta_ref[...], g0, b0).astype(BF16)
        hs_ref[BLOCK:, :] = _ln(x_ref[0], g0, b0).astype(BF16)
        hs = hs_ref[...]
        nkv = 2 * W_KA
        kva_ref[0] = (_dot(hs, wkv_ref[:, :nkv]) + bkv_ref[:, :nkv]).astype(BF16)
        vb_ref[0, :P, :] = (_dot(hs, wkv_ref[:, nkv:]) + bkv_ref[:, nkv:]).astype(BF16)
        vb_ref[0, P:, :] = jnp.zeros((vb_ref.shape[1] - P, vb_ref.shape[2]), BF16)
        kbt_ref[0, :, :P] = (_dot_nt(wkbt_ref[...], hs) + bkbt_ref[...]).astype(BF16)
        kbt_ref[0, :, P:] = jnp.zeros((kbt_ref.shape[1], kbt_ref.shape[2] - P), BF16)
        c_ref[0, :, P:] = jnp.full((c_ref.shape[1], c_ref.shape[2] - P), C_MASKED, F32)
        fbt = _dot_nt(wfbt_ref[...], hs) + bfbt_ref[...]
        lf = jax.nn.log_sigmoid(fbt)
        pos = lax.broadcasted_iota(jnp.int32, lf.shape, 1)
        lf = jnp.where(pos >= PAD, lf, 0.0)
        ri = lax.broadcasted_iota(jnp.int32, (BLOCK, BLOCK), 0)
        ci = lax.broadcasted_iota(jnp.int32, (BLOCK, BLOCK), 1)
        tri = jnp.where(ri <= ci, 1.0, 0.0).astype(BF16)
        carry = jnp.zeros((lf.shape[0], 1), F32)
        for n in range(P // BLOCK):
            hi, mid, lo = _split3(lf[:, n * BLOCK:(n + 1) * BLOCK])
            cb = _dot(hi, tri) + _dot(mid, tri) + _dot(lo, tri) + carry
            carry = cb[:, BLOCK - 1:BLOCK]
            if n == 0:
                cb = jnp.where(lax.broadcasted_iota(jnp.int32, cb.shape, 1) >= PAD, cb, C_MASKED)
            c_ref[0, :, n * BLOCK:(n + 1) * BLOCK] = cb[:N_HEADS_B]

    @pl.when(j > 0)
    def _():
        zq_ref[0] = (_dot(hs_ref[BLOCK:, :], wq_ref[...]) + bq_ref[...]).astype(BF16)


def _inproj(x, meta, g0, b0, wkv, bkv, wkbt, bkbt, wfbt, bfbt, wq, bq):
    B, S, D = x.shape
    P = S + BLOCK
    P2 = -(-P // FOX_TK) * FOX_TK
    nq = wq.shape[1] // QT_COLS
    full = lambda shape: pl.BlockSpec(shape, lambda b, j: (0,) * len(shape))
    qcol = lambda b, j: (0, jnp.maximum(j - 1, 0))
    return pl.pallas_call(
        _inproj_kernel,
        grid=(B, nq + 1),
        in_specs=[
            pl.BlockSpec((1, S, D), lambda b, j: (b, 0, 0)),
            full(meta.shape), full(g0.shape), full(b0.shape),
            full(wkv.shape), full(bkv.shape), full(wkbt.shape), full(bkbt.shape),
            full(wfbt.shape), full(bfbt.shape),
            pl.BlockSpec((D, QT_COLS), qcol),
            pl.BlockSpec((1, QT_COLS), qcol),
        ],
        out_specs=[
            pl.BlockSpec((1, P, 2 * W_KA), lambda b, j: (b, 0, 0)),
            pl.BlockSpec((1, P2, W_B), lambda b, j: (b, 0, 0)),
            pl.BlockSpec((1, W_B, P2), lambda b, j: (b, 0, 0)),
            pl.BlockSpec((1, N_HEADS_B, P2), lambda b, j: (b, 0, 0)),
            pl.BlockSpec((1, S, QT_COLS), lambda b, j: (b, 0, jnp.maximum(j - 1, 0))),
        ],
        out_shape=[
            jax.ShapeDtypeStruct((B, P, 2 * W_KA), BF16),
            jax.ShapeDtypeStruct((B, P2, W_B), BF16),
            jax.ShapeDtypeStruct((B, W_B, P2), BF16),
            jax.ShapeDtypeStruct((B, N_HEADS_B, P2), F32),
            jax.ShapeDtypeStruct((B, S, wq.shape[1]), BF16),
        ],
        scratch_shapes=[pltpu.VMEM((P, D), BF16)],
        compiler_params=pltpu.CompilerParams(
            dimension_semantics=("arbitrary", "arbitrary"), vmem_limit_bytes=VMEM_LIMIT),
        name="inproj",
    )(x, meta, g0, b0, wkv, bkv, wkbt, bkbt, wfbt, bfbt, wq, bq)


def _t5_bucket_table():
    qi = np.arange(BLOCK)[:, None] + BLOCK
    kj = np.arange(2 * BLOCK)[None, :]
    n = np.maximum(qi - kj, 0)
    max_exact = T5_BUCKETS // 2
    nf = np.maximum(n, 1).astype(np.float32)
    large = max_exact + (np.log(nf / np.float32(max_exact)) / np.float32(math.log(T5_MAX_DIST / max_exact))
                         * (T5_BUCKETS - max_exact)).astype(np.int32)
    large = np.minimum(large, T5_BUCKETS - 1)
    bucket = np.where(n < max_exact, n, large).astype(np.int32)
    dist = qi - kj
    window = (dist >= 0) & (dist < WINDOW)
    return np.where(window, bucket, -1).astype(np.int32)


def _swa_kernel(bucket_ref, rel_ref, sink_ref, q_ref, kvp_ref, kvc_ref, o_ref, bias_ref):
    b = pl.program_id(0)
    n = pl.program_id(1)

    @pl.when((b == 0) & (n == 0))
    def _():
        bucket = bucket_ref[...]
        for h in range(N_HEADS_A):
            acc = jnp.where(bucket < 0, NEG, 0.0).astype(F32)
            for t in range(T5_BUCKETS):
                acc = jnp.where(bucket == t, rel_ref[t, h], acc)
            bias_ref[h] = acc

    kcol = lax.broadcasted_iota(jnp.int32, (BLOCK, 2 * BLOCK), 1)
    kabs_ok = (n * BLOCK + kcol) >= PAD
    q = q_ref[0]
    kvp = kvp_ref[0]
    kvc = kvc_ref[0]
    outs = []
    for h in range(N_HEADS_A):
        g = h // GROUP_A
        k = jnp.concatenate([kvp[:, g * HEAD_DIM:(g + 1) * HEAD_DIM],
                             kvc[:, g * HEAD_DIM:(g + 1) * HEAD_DIM]], axis=0)
        v = jnp.concatenate([kvp[:, W_KA + g * HEAD_DIM:W_KA + (g + 1) * HEAD_DIM],
                             kvc[:, W_KA + g * HEAD_DIM:W_KA + (g + 1) * HEAD_DIM]], axis=0)
        s = _dot_nt(q[:, h * HEAD_DIM:(h + 1) * HEAD_DIM], k) + bias_ref[h]
        s = jnp.where(kabs_ok, s, NEG)
        sink = sink_ref[h]
        m = jnp.maximum(jnp.max(s, axis=-1, keepdims=True), sink)
        p = jnp.exp(s - m)
        denom = jnp.sum(p, axis=-1, keepdims=True) + jnp.exp(sink - m)
        o = _dot(p.astype(BF16), v)
        outs.append(o / denom)
    o_ref[0] = jnp.concatenate(outs, axis=-1).astype(BF16)


def _swa(zq, kva, rel_bias, sinks):
    B, S, _ = zq.shape
    nb = S // BLOCK
    bucket = jnp.asarray(_t5_bucket_table())
    smem = pl.BlockSpec(memory_space=pltpu.SMEM)
    return pl.pallas_call(
        _swa_kernel,
        grid=(B, nb),
        in_specs=[
            pl.BlockSpec((BLOCK, 2 * BLOCK), lambda b, n: (0, 0)),
            smem, smem,
            pl.BlockSpec((1, BLOCK, W_QA), lambda b, n: (b, n, 0)),
            pl.BlockSpec((1, BLOCK, 2 * W_KA), lambda b, n: (b, n, 0)),
            pl.BlockSpec((1, BLOCK, 2 * W_KA), lambda b, n: (b, n + 1, 0)),
        ],
        out_specs=pl.BlockSpec((1, BLOCK, W_QA), lambda b, n: (b, n, 0)),
        out_shape=jax.ShapeDtypeStruct((B, S, W_QA), BF16),
        scratch_shapes=[pltpu.VMEM((N_HEADS_A, BLOCK, 2 * BLOCK), F32)],
        compiler_params=pltpu.CompilerParams(
            dimension_semantics=("arbitrary", "arbitrary"), vmem_limit_bytes=VMEM_LIMIT),
        name="swa",
    )(bucket, rel_bias, sinks, zq, kva, kva)


FOX_TQ = 256
FOX_TK = 512
C_MASKED = 1e30


def _fox_kernel(q_ref, kt_ref, v_ref, c_ref, o_ref):
    pair = pl.program_id(1)
    qi = pl.program_id(2)
    tq = q_ref.shape[1]
    tk = FOX_TK
    q0 = BLOCK + qi * tq
    q = q_ref[0]
    first = lax.broadcasted_iota(jnp.int32, q.shape, 1) < HEAD_DIM
    qzero = jnp.zeros_like(q)
    qm = (jnp.where(first, q, qzero), jnp.where(first, qzero, q))
    qpos = q0 + lax.broadcasted_iota(jnp.int32, (tq, tk), 0)
    kcol = lax.broadcasted_iota(jnp.int32, (tq, tk), 1)

    def crows(start, size):
        c8 = c_ref[0, :, pl.ds(start, size)]
        hrow = lax.broadcasted_iota(jnp.int32, c8.shape, 0)
        return tuple(jnp.sum(jnp.where(hrow == 2 * pair + i, c8, 0.0), axis=0, keepdims=True)
                     for i in range(2))

    cq0 = tuple(r[:, 0:1] for r in crows(pl.multiple_of(q0, BLOCK), BLOCK))

    def chunk(jc, carry, causal):
        m, l, acc = carry
        ks = pl.multiple_of(jc * tk, tk)
        kt = kt_ref[0, :, pl.ds(ks, tk)]
        vv = v_ref[0, pl.ds(ks, tk), :]
        vfirst = lax.broadcasted_iota(jnp.int32, vv.shape, 1) < HEAD_DIM
        vzero = jnp.zeros_like(vv)
        vm = (jnp.where(vfirst, vv, vzero), jnp.where(vfirst, vzero, vv))
        cr = crows(ks, tk)
        m_out, l_out, alphas, pv = [], [], [], None
        for i in range(2):
            s = _dot(qm[i], kt) + (cq0[i] - cr[i])
            if causal:
                s = jnp.where(ks + kcol <= qpos, s, NEG)
            m_new = jnp.maximum(m[i], jnp.max(s, axis=-1, keepdims=True))
            alpha = jnp.exp(m[i] - m_new)
            p = jnp.exp(s - m_new)
            m_out.append(m_new)
            l_out.append(alpha * l[i] + jnp.sum(p, axis=-1, keepdims=True))
            alphas.append(alpha)
            d = _dot(p.astype(BF16), vm[i])
            pv = d if pv is None else pv + d
        acc = jnp.where(first, alphas[0], alphas[1]) * acc + pv
        return tuple(m_out), tuple(l_out), acc

    n_full = q0 // tk
    n_chunks = (q0 + tq + tk - 1) // tk
    neg = jnp.full((tq, 1), NEG, F32)
    zero = jnp.zeros((tq, 1), F32)
    carry = ((neg, neg), (zero, zero), jnp.zeros((tq, 2 * HEAD_DIM), F32))
    carry = lax.fori_loop(0, n_full, functools.partial(chunk, causal=False), carry)
    m, l, acc = lax.fori_loop(n_full, n_chunks, functools.partial(chunk, causal=True), carry)
    o_ref[0] = (acc / jnp.where(first, l[0], l[1])).astype(BF16)


def _fox(zq, kbt, vb, c):
    B, S, _ = zq.shape
    P = vb.shape[1]
    npair = N_HEADS_B // 2
    qblk0 = W_QA // (2 * HEAD_DIM)
    return pl.pallas_call(
        _fox_kernel,
        grid=(B, npair, S // FOX_TQ),
        in_specs=[
            pl.BlockSpec((1, FOX_TQ, 2 * HEAD_DIM), lambda b, p, i: (b, i, qblk0 + p)),
            pl.BlockSpec((1, 2 * HEAD_DIM, P), lambda b, p, i: (b, p, 0)),
            pl.BlockSpec((1, P, 2 * HEAD_DIM), lambda b, p, i: (b, 0, p)),
            pl.BlockSpec((1, N_HEADS_B, P), lambda b, p, i: (b, 0, 0)),
        ],
        out_specs=pl.BlockSpec((1, FOX_TQ, 2 * HEAD_DIM), lambda b, p, i: (b, i, p)),
        out_shape=jax.ShapeDtypeStruct((B, S, W_B), BF16),
        compiler_params=pltpu.CompilerParams(
            dimension_semantics=("arbitrary", "arbitrary", "arbitrary"), vmem_limit_bytes=VMEM_LIMIT),
        name="fox",
    )(zq, kbt, vb, c)


MERGE_TM = 512


def _merge_kernel(x_ref, oa_ref, ob_ref, ga_ref, gb_ref, g0_ref, b0_ref, wpa_ref, wpb_ref, wo_ref,
                  g1_ref, b1_ref, h1_ref, h1t_ref):
    h = _ln(x_ref[...], g0_ref[...], b0_ref[...])
    ya = _dot(oa_ref[...], wpa_ref[...])
    yb = _dot(ob_ref[...], wpb_ref[...])
    merged = jax.nn.sigmoid(ga_ref[...].astype(F32)) * ya + jax.nn.sigmoid(gb_ref[...].astype(F32)) * yb
    y = _dot(merged.astype(BF16), wo_ref[...])
    h1 = _ln(DEEPNORM_ALPHA * h + y, g1_ref[...], b1_ref[...])
    h1_ref[...] = h1
    h1t_ref[...] = h1.T.astype(BF16)


def _merge(x2, oa2, ob2, zq2, g0, b0, wpa, wpb, wo, g1, b1):
    T, D = x2.shape
    tm = MERGE_TM
    full = lambda a: pl.BlockSpec(a.shape, lambda i: (0,) * a.ndim)
    gate0 = (W_QA + W_B) // D
    return pl.pallas_call(
        _merge_kernel,
        grid=(T // tm,),
        in_specs=[
            pl.BlockSpec((tm, D), lambda i: (i, 0)),
            pl.BlockSpec((tm, W_QA), lambda i: (i, 0)),
            pl.BlockSpec((tm, W_B), lambda i: (i, 0)),
            pl.BlockSpec((tm, D), lambda i: (i, gate0)),
            pl.BlockSpec((tm, D), lambda i: (i, gate0 + 1)),
            full(g0), full(b0), full(wpa), full(wpb), full(wo), full(g1), full(b1),
        ],
        out_specs=[pl.BlockSpec((tm, D), lambda i: (i, 0)), pl.BlockSpec((D, tm), lambda i: (0, i))],
        out_shape=[jax.ShapeDtypeStruct((T, D), F32), jax.ShapeDtypeStruct((D, T), BF16)],
        compiler_params=pltpu.CompilerParams(
            dimension_semantics=("arbitrary",), vmem_limit_bytes=VMEM_LIMIT),
        name="merge",
    )(x2, oa2, ob2, zq2, zq2, g0, b0, wpa, wpb, wo, g1, b1)


ROUTE_TM = 512
LANES = 128
HALF_DQ = PEER_DQ // 2
RANK_OUT = 64.0


def _top_sorted(s):
    rank = jnp.full(s.shape, RANK_OUT, F32)
    tops = []
    for r in range(PEER_TOPK):
        m = jnp.max(s, axis=0, keepdims=True)
        eq = s == m
        rank = jnp.where(eq, float(r), rank)
        s = jnp.where(eq, -jnp.inf, s)
        tops.append(m)
    return tops, rank


def _stack_rows(rows):
    n = len(rows)
    ri = lax.broadcasted_iota(jnp.int32, (n, rows[0].shape[1]), 0)
    out = jnp.broadcast_to(rows[0], ri.shape)
    for r in range(1, n):
        out = jnp.where(ri == r, rows[r], out)
    return out


def _route_kernel(ht_ref, wqt_ref, sk_ref, na_ref, e1_ref, rb_ref, e2_ref, q_ref, sc_ref):
    K = PEER_TOPK
    q_ref[...] = _dot(wqt_ref[...], ht_ref[...]).astype(BF16)
    for h in range(PEER_HEADS):
        sc_ref[h] = _dot(sk_ref[h], q_ref[h * PEER_DQ:(h + 1) * PEER_DQ, :])

    n_groups = ht_ref.shape[1] // LANES

    def group(idx, _):
        h = idx // n_groups
        g = idx % n_groups
        lanes = pl.ds(pl.multiple_of(g * LANES, LANES), LANES)
        s1 = sc_ref[h, :PEER_NKEYS, lanes]
        s2 = sc_ref[h, PEER_NKEYS:, lanes]
        a, ra = _top_sorted(s1)
        b, rb = _top_sorted(s2)
        bmat16 = _stack_rows(b)
        bmat8 = bmat16[:8]
        cands = [a[0] + bmat16]
        for i in range(1, 8):
            cands.append(a[i] + bmat8)
        cands.append(_stack_rows(a[8:]) + b[0])
        cand = jnp.concatenate(cands, axis=0)
        vmax = a[0] + b[0]
        z = jnp.zeros_like(vmax)
        for r in range(K):
            m = jnp.max(cand, axis=0, keepdims=True)
            cand = jnp.where(cand == m, -jnp.inf, cand)
            z = z + jnp.exp(m - vmax)
        sel = jnp.where(cand == -jnp.inf, 1.0, 0.0)
        counts = [jnp.sum(sel[0:16], axis=0, keepdims=True)]
        for i in range(1, 8):
            counts.append(jnp.sum(sel[8 + 8 * i:16 + 8 * i], axis=0, keepdims=True))
        n_tail = jnp.sum(sel[72:80], axis=0, keepdims=True)
        na = jnp.where(ra < 8.0 + n_tail, 1.0, 0.0)
        for i in range(8):
            na = jnp.where(ra == float(i), counts[i], na)
        na_ref[h, :, lanes] = na
        e1_ref[h, :, lanes] = jnp.exp(s1 - a[0])
        rb_ref[h, :, lanes] = rb.astype(BF16)
        e2_ref[h, :, lanes] = (jnp.exp(s2 - b[0]) * (0.5 / z)).astype(BF16)
        return 0

    lax.fori_loop(0, PEER_HEADS * n_groups, group, 0)


def _route(h1t, wqt, skbd):
    D, T = h1t.shape
    tm = ROUTE_TM
    full = lambda a: pl.BlockSpec(a.shape, lambda i: (0,) * a.ndim)
    ospec = pl.BlockSpec((PEER_HEADS, PEER_NKEYS, tm), lambda i: (0, 0, i))
    oshape = lambda dt: jax.ShapeDtypeStruct((PEER_HEADS, PEER_NKEYS, T), dt)
    return pl.pallas_call(
        _route_kernel,
        grid=(T // tm,),
        in_specs=[pl.BlockSpec((D, tm), lambda i: (0, i)), full(wqt), full(skbd)],
        out_specs=[ospec, ospec, ospec, ospec],
        out_shape=[oshape(F32), oshape(F32), oshape(BF16), oshape(BF16)],
        scratch_shapes=[pltpu.VMEM((PEER_HEADS * PEER_DQ, tm), BF16),
                        pltpu.VMEM((PEER_HEADS, 2 * PEER_NKEYS, tm), F32)],
        compiler_params=pltpu.CompilerParams(
            dimension_semantics=("arbitrary",), vmem_limit_bytes=VMEM_LIMIT),
        name="route",
    )(h1t, wqt, skbd)


EXP_TM = 512
EXP_TE = 2048
EXP_IA = EXP_TE // PEER_NKEYS
BF16_ROWS = 16
SQRT_HALF = 0.7071067811865476


def _expert_kernel(xt_ref, h1_ref, u_ref, v_ref, na_ref, e1_ref, rb_ref, e2_ref, g2_ref, b2_ref,
                   o_ref, acc_ref, hu_ref, a_ref):
    j = pl.program_id(1)
    nj = pl.num_programs(1)
    tm = xt_ref.shape[1]
    nsub = PEER_NKEYS // BF16_ROWS

    @pl.when(j == 0)
    def _():
        acc_ref[...] = jnp.zeros_like(acc_ref)

    hu_ref[...] = _dot(u_ref[...], xt_ref[...])
    rows = pl.ds(pl.multiple_of(j * EXP_IA, EXP_IA), EXP_IA)
    for il in range(EXP_IA):
        sl = slice(il * PEER_NKEYS, (il + 1) * PEER_NKEYS)
        hu = hu_ref[sl, :]
        act = (hu * (1.0 + lax.erf(hu * SQRT_HALF))).astype(BF16)
        gate = jnp.zeros((nsub, BF16_ROWS, tm), BF16)
        for h in range(PEER_HEADS):
            na = jnp.broadcast_to(na_ref[h, rows, :][il:il + 1], (BF16_ROWS, tm)).astype(BF16)
            e1 = jnp.broadcast_to(e1_ref[h, rows, :][il:il + 1], (BF16_ROWS, tm)).astype(BF16)
            rb = rb_ref[h].reshape(nsub, BF16_ROWS, tm)
            w = e2_ref[h].reshape(nsub, BF16_ROWS, tm) * e1[None]
            gate = gate + jnp.where(rb < na[None], w, jnp.zeros_like(w))
        a_ref[sl, :] = act * gate.reshape(PEER_NKEYS, tm)
    acc_ref[...] += _dot_tn(a_ref[...], v_ref[...])

    @pl.when(j == nj - 1)
    def _():
        o_ref[...] = _ln(DEEPNORM_ALPHA * h1_ref[...] + acc_ref[...], g2_ref[...], b2_ref[...])


def _experts(h1t, h1, u, v, na, e1, rb, e2, g2, b2):
    T, D = h1.shape
    E = u.shape[0]
    tm, te = EXP_TM, EXP_TE
    gspec = pl.BlockSpec((PEER_HEADS, PEER_NKEYS, tm), lambda i, j: (0, 0, i))
    vec = pl.BlockSpec((1, D), lambda i, j: (0, 0))
    return pl.pallas_call(
        _expert_kernel,
        grid=(T // tm, E // te),
        in_specs=[
            pl.BlockSpec((D, tm), lambda i, j: (0, i)),
            pl.BlockSpec((tm, D), lambda i, j: (i, 0)),
            pl.BlockSpec((te, D), lambda i, j: (j, 0)),
            pl.BlockSpec((te, D), lambda i, j: (j, 0)),
            gspec, gspec, gspec, gspec, vec, vec,
        ],
        out_specs=pl.BlockSpec((tm, D), lambda i, j: (i, 0)),
        out_shape=jax.ShapeDtypeStruct((T, D), F32),
        scratch_shapes=[pltpu.VMEM((tm, D), F32), pltpu.VMEM((te, tm), F32), pltpu.VMEM((te, tm), BF16)],
        compiler_params=pltpu.CompilerParams(
            dimension_semantics=("arbitrary", "arbitrary"), vmem_limit_bytes=VMEM_LIMIT),
        name="experts",
    )(h1t, h1, u, v, na, e1, rb, e2, g2, b2)


def kernel(x, meta_tokens, emb_ln_g, emb_ln_b, rel_bias, w_in, b_in, attn_sinks, w_pa, w_pb, w_o,
           ln1_g, ln1_b, peer_wq, peer_subkeys, peer_u, peer_v, ln2_g, ln2_b):
    B, S, D = x.shape
    T = B * S
    row = lambda a: a.reshape(1, -1).astype(F32)

    w, bi = w_in[0], b_in[0]
    sizes = (W_QA, W_KA, W_KA, W_B, W_B, W_B, N_HEADS_B, D, D)
    offs = np.cumsum((0,) + sizes)
    col = lambda a, k: a[..., offs[k]:offs[k + 1]]
    scale = HEAD_DIM ** -0.5
    wq = jnp.concatenate([col(w, 0) * scale, col(w, 3) * scale, col(w, 7), col(w, 8)], axis=-1).astype(BF16)
    bq = jnp.concatenate([col(bi, 0) * scale, col(bi, 3) * scale, col(bi, 7), col(bi, 8)], axis=-1)
    wkv = jnp.concatenate([col(w, 1), col(w, 2), col(w, 5)], axis=-1).astype(BF16)
    bkv = jnp.concatenate([col(bi, 1), col(bi, 2), col(bi, 5)], axis=-1)
    wkbt = col(w, 4).T.astype(BF16)
    bkbt = col(bi, 4).reshape(-1, 1).astype(F32)
    fpad = 16 - N_HEADS_B
    wfbt = jnp.pad(col(w, 6).T, ((0, fpad), (0, 0))).astype(BF16)
    bfbt = jnp.pad(col(bi, 6), (0, fpad)).reshape(-1, 1).astype(F32)

    g0, b0 = row(emb_ln_g), row(emb_ln_b)
    kva, vb, kbt, c, zq = _inproj(x, meta_tokens.astype(F32), g0, b0, wkv, row(bkv), wkbt, bkbt,
                                  wfbt, bfbt, wq, row(bq))
    oa = _swa(zq, kva, rel_bias.astype(F32), attn_sinks[0].astype(F32))
    ob = _fox(zq, kbt, vb, c)

    h1, h1t = _merge(x.reshape(T, D), oa.reshape(T, W_QA), ob.reshape(T, W_B), zq.reshape(T, -1),
                     g0, b0, w_pa[0].astype(BF16), w_pb[0].astype(BF16), w_o[0].astype(BF16),
                     row(ln1_g[0]), row(ln1_b[0]))

    wqt = peer_wq[0].T.astype(BF16)
    sk = peer_subkeys[0].astype(BF16)
    zero = jnp.zeros_like(sk[:, 0])
    skbd = jnp.concatenate([jnp.concatenate([sk[:, 0], zero], axis=-1),
                            jnp.concatenate([zero, sk[:, 1]], axis=-1)], axis=1)
    na, e1, rb, e2 = _route(h1t, wqt, skbd)
    out = _experts(h1t, h1, peer_u[0].astype(BF16), peer_v[0].astype(BF16), na, e1, rb, e2,
                   row(ln2_g[0]), row(ln2_b[0]))
    return out.reshape(B, S, D)
```

```python
import functools
import math

import jax
import jax.numpy as jnp
import numpy as np
from jax import lax
from jax.experimental import pallas as pl
from jax.experimental.pallas import tpu as pltpu

N_META = 16
BLOCK = 128
PAD = BLOCK - N_META
HEAD_DIM = 64
N_HEADS_A = 8
N_KV_A = 2
GROUP_A = N_HEADS_A // N_KV_A
WINDOW = 128
N_HEADS_B = 8
T5_BUCKETS = 32
T5_MAX_DIST = 128
PEER_HEADS = 8
PEER_NKEYS = 128
PEER_DQ = 256
PEER_TOPK = 16
LN_EPS = 1e-5
NEG = -1e30
LOG2E = 1.4426950408889634
DEPTH = 1
DEEPNORM_ALPHA = (2 * DEPTH) ** 0.25

W_QA = N_HEADS_A * HEAD_DIM
W_KA = N_KV_A * HEAD_DIM
W_B = N_HEADS_B * HEAD_DIM
KVA_W = 4 * W_KA

F32 = jnp.float32
BF16 = jnp.bfloat16

VMEM_LIMIT = 56 * 1024 * 1024


def _ln(x, g, b):
    mu = jnp.mean(x, axis=-1, keepdims=True)
    xc = x - mu
    var = jnp.mean(xc * xc, axis=-1, keepdims=True)
    return xc * lax.rsqrt(var + LN_EPS) * g + b


def _dot(a, b):
    return jnp.dot(a, b, preferred_element_type=F32)


def _dot_nt(a, b):
    return lax.dot_general(a, b, (((1,), (1,)), ((), ())), preferred_element_type=F32)


def _dot_tn(a, b):
    return lax.dot_general(a, b, (((0,), (0,)), ((), ())), preferred_element_type=F32)


def _split3(x):
    hi = x.astype(BF16)
    r = x - hi.astype(F32)
    mid = r.astype(BF16)
    lo = (r - mid.astype(F32)).astype(BF16)
    return hi, mid, lo


QT_COLS = 512


def _inproj_kernel(x_ref, meta_ref, g0_ref, b0_ref, wkv_ref, bkv_ref, wkbt_ref, bkbt_ref,
                   wfbt_ref, bfbt_ref, wq_ref, bq_ref,
                   kva_ref, vb_ref, kbt_ref, c_ref, zq_ref, hs_ref):
    j = pl.program_id(1)
    P = hs_ref.shape[0]

    @pl.when(j == 0)
    def _():
        g0 = g0_ref[...]
        b0 = b0_ref[...]
        hs_ref[0:PAD, :] = jnp.zeros((PAD, hs_ref.shape[1]), BF16)
        hs_ref[PAD:BLOCK, :] = _ln(meta_ref[...], g0, b0).astype(BF16)
        hs_ref[BLOCK:, :] = _ln(x_ref[0], g0, b0).astype(BF16)
        hs = hs_ref[...]
        nkv = KVA_W
        kva_ref[0] = (_dot(hs, wkv_ref[:, :nkv]) + bkv_ref[:, :nkv]).astype(BF16)
        vb_ref[0, :P, :] = (_dot(hs, wkv_ref[:, nkv:]) + bkv_ref[:, nkv:]).astype(BF16)
        vb_ref[0, P:, :] = jnp.zeros((vb_ref.shape[1] - P, vb_ref.shape[2]), BF16)
        kbt_ref[0, :, :P] = (_dot_nt(wkbt_ref[...], hs) + bkbt_ref[...]).astype(BF16)
        kbt_ref[0, :, P:] = jnp.zeros((kbt_ref.shape[1], kbt_ref.shape[2] - P), BF16)
        c_ref[0, :, P:] = jnp.full((c_ref.shape[1], c_ref.shape[2] - P), C_MASKED, F32)
        fbt = _dot_nt(wfbt_ref[...], hs) + bfbt_ref[...]
        lf = jax.nn.log_sigmoid(fbt)
        pos = lax.broadcasted_iota(jnp.int32, lf.shape, 1)
        lf = jnp.where(pos >= PAD, lf, 0.0)
        ri = lax.broadcasted_iota(jnp.int32, (BLOCK, BLOCK), 0)
        ci = lax.broadcasted_iota(jnp.int32, (BLOCK, BLOCK), 1)
        tri = jnp.where(ri <= ci, 1.0, 0.0).astype(BF16)
        carry = jnp.zeros((lf.shape[0], 1), F32)
        for n in range(P // BLOCK):
            hi, mid, lo = _split3(lf[:, n * BLOCK:(n + 1) * BLOCK])
            cb = _dot(hi, tri) + _dot(mid, tri) + _dot(lo, tri) + carry
            carry = cb[:, BLOCK - 1:BLOCK]
            if n == 0:
                cb = jnp.where(lax.broadcasted_iota(jnp.int32, cb.shape, 1) >= PAD, cb, C_MASKED)
            c_ref[0, :, n * BLOCK:(n + 1) * BLOCK] = cb[:N_HEADS_B]

    @pl.when(j > 0)
    def _():
        zq_ref[0] = (_dot(hs_ref[BLOCK:, :], wq_ref[...]) + bq_ref[...]).astype(BF16)


def _inproj(x, meta, g0, b0, wkv, bkv, wkbt, bkbt, wfbt, bfbt, wq, bq):
    B, S, D = x.shape
    P = S + BLOCK
    P2 = -(-P // FOX_TK) * FOX_TK
    nq = wq.shape[1] // QT_COLS
    full = lambda shape: pl.BlockSpec(shape, lambda b, j: (0,) * len(shape))
    qcol = lambda b, j: (0, jnp.maximum(j - 1, 0))
    return pl.pallas_call(
        _inproj_kernel,
        grid=(B, nq + 1),
        in_specs=[
            pl.BlockSpec((1, S, D), lambda b, j: (b, 0, 0)),
            full(meta.shape), full(g0.shape), full(b0.shape),
            full(wkv.shape), full(bkv.shape), full(wkbt.shape), full(bkbt.shape),
            full(wfbt.shape), full(bfbt.shape),
            pl.BlockSpec((D, QT_COLS), qcol),
            pl.BlockSpec((1, QT_COLS), qcol),
        ],
        out_specs=[
            pl.BlockSpec((1, P, KVA_W), lambda b, j: (b, 0, 0)),
            pl.BlockSpec((1, P2, W_B), lambda b, j: (b, 0, 0)),
            pl.BlockSpec((1, W_B, P2), lambda b, j: (b, 0, 0)),
            pl.BlockSpec((1, N_HEADS_B, P2), lambda b, j: (b, 0, 0)),
            pl.BlockSpec((1, S, QT_COLS), lambda b, j: (b, 0, jnp.maximum(j - 1, 0))),
        ],
        out_shape=[
            jax.ShapeDtypeStruct((B, P, KVA_W), BF16),
            jax.ShapeDtypeStruct((B, P2, W_B), BF16),
            jax.ShapeDtypeStruct((B, W_B, P2), BF16),
            jax.ShapeDtypeStruct((B, N_HEADS_B, P2), F32),
            jax.ShapeDtypeStruct((B, S, wq.shape[1]), BF16),
        ],
        scratch_shapes=[pltpu.VMEM((P, D), BF16)],
        compiler_params=pltpu.CompilerParams(
            dimension_semantics=("arbitrary", "arbitrary"), vmem_limit_bytes=VMEM_LIMIT),
        name="inproj",
    )(x, meta, g0, b0, wkv, bkv, wkbt, bkbt, wfbt, bfbt, wq, bq)


def _t5_bucket_table():
    qi = np.arange(BLOCK)[:, None] + BLOCK
    kj = np.arange(2 * BLOCK)[None, :]
    n = np.maximum(qi - kj, 0)
    max_exact = T5_BUCKETS // 2
    nf = np.maximum(n, 1).astype(np.float32)
    large = max_exact + (np.log(nf / np.float32(max_exact)) / np.float32(math.log(T5_MAX_DIST / max_exact))
                         * (T5_BUCKETS - max_exact)).astype(np.int32)
    large = np.minimum(large, T5_BUCKETS - 1)
    bucket = np.where(n < max_exact, n, large).astype(np.int32)
    dist = qi - kj
    window = (dist >= 0) & (dist < WINDOW)
    return np.where(window, bucket, -1).astype(np.int32)


def _swa_kernel(bucket_ref, rel_ref, sink_ref, q_ref, kvp_ref, kvc_ref, o_ref, bias_ref):
    b = pl.program_id(0)
    n = pl.program_id(1)

    @pl.when((b == 0) & (n == 0))
    def _():
        bucket = bucket_ref[...]
        for h in range(N_HEADS_A):
            acc = jnp.where(bucket < 0, NEG, 0.0).astype(F32)
            for t in range(T5_BUCKETS):
                acc = jnp.where(bucket == t, rel_ref[t, h] * LOG2E, acc)
            bias_ref[h] = acc

    kcol = lax.broadcasted_iota(jnp.int32, (BLOCK, 2 * BLOCK), 1)
    kabs_ok = (n * BLOCK + kcol) >= PAD
    pw = 2 * HEAD_DIM
    first = lax.broadcasted_iota(jnp.int32, (BLOCK, pw), 1) < HEAD_DIM
    vfirst = lax.broadcasted_iota(jnp.int32, (2 * BLOCK, pw), 1) < HEAD_DIM
    for g in range(N_KV_A):
        kk = jnp.concatenate([kvp_ref[0, :, g * pw:(g + 1) * pw],
                              kvc_ref[0, :, g * pw:(g + 1) * pw]], axis=0)
        vv = jnp.concatenate([kvp_ref[0, :, (N_KV_A + g) * pw:(N_KV_A + g + 1) * pw],
                              kvc_ref[0, :, (N_KV_A + g) * pw:(N_KV_A + g + 1) * pw]], axis=0)
        vzero = jnp.zeros_like(vv)
        vm = (jnp.where(vfirst, vv, vzero), jnp.where(vfirst, vzero, vv))
        for pp in range(GROUP_A // 2):
            pair = g * (GROUP_A // 2) + pp
            q = q_ref[0, :, pair * pw:(pair + 1) * pw]
            qzero = jnp.zeros_like(q)
            qm = (jnp.where(first, q, qzero), jnp.where(first, qzero, q))
            out = None
            for i in range(2):
                h = 2 * pair + i
                s = _dot_nt(qm[i], kk) + bias_ref[h]
                s = jnp.where(kabs_ok, s, NEG)
                sink = sink_ref[h] * LOG2E
                m = jnp.maximum(jnp.max(s, axis=-1, keepdims=True), sink)
                p = jnp.exp2(s - m)
                denom = jnp.sum(p, axis=-1, keepdims=True) + jnp.exp2(sink - m)
                o = _dot(p.astype(BF16), vm[i]) / denom
                out = o if out is None else out + o
            o_ref[0, :, pair * pw:(pair + 1) * pw] = out.astype(BF16)


def _swa(zq, kva, rel_bias, sinks):
    B, S, _ = zq.shape
    nb = S // BLOCK
    bucket = jnp.asarray(_t5_bucket_table())
    smem = pl.BlockSpec(memory_space=pltpu.SMEM)
    return pl.pallas_call(
        _swa_kernel,
        grid=(B, nb),
        in_specs=[
            pl.BlockSpec((BLOCK, 2 * BLOCK), lambda b, n: (0, 0)),
            smem, smem,
            pl.BlockSpec((1, BLOCK, W_QA), lambda b, n: (b, n, 0)),
            pl.BlockSpec((1, BLOCK, KVA_W), lambda b, n: (b, n, 0)),
            pl.BlockSpec((1, BLOCK, KVA_W), lambda b, n: (b, n + 1, 0)),
        ],
        out_specs=pl.BlockSpec((1, BLOCK, W_QA), lambda b, n: (b, n, 0)),
        out_shape=jax.ShapeDtypeStruct((B, S, W_QA), BF16),
        scratch_shapes=[pltpu.VMEM((N_HEADS_A, BLOCK, 2 * BLOCK), F32)],
        compiler_params=pltpu.CompilerParams(
            dimension_semantics=("arbitrary", "arbitrary"), vmem_limit_bytes=VMEM_LIMIT),
        name="swa",
    )(bucket, rel_bias, sinks, zq, kva, kva)


FOX_TQ = 256
FOX_TK = 512
C_MASKED = 1e30


def _fox_kernel(q_ref, kt_ref, v_ref, c_ref, o_ref):
    pair = pl.program_id(1)
    qi = pl.program_id(2)
    tq = q_ref.shape[1]
    tk = FOX_TK
    q0 = BLOCK + qi * tq
    q = q_ref[0]
    first = lax.broadcasted_iota(jnp.int32, q.shape, 1) < HEAD_DIM
    qzero = jnp.zeros_like(q)
    qm = (jnp.where(first, q, qzero), jnp.where(first, qzero, q))
    qpos = q0 + lax.broadcasted_iota(jnp.int32, (tq, tk), 0)
    kcol = lax.broadcasted_iota(jnp.int32, (tq, tk), 1)

    def crows(start, size):
        c8 = c_ref[0, :, pl.ds(start, size)]
        hrow = lax.broadcasted_iota(jnp.int32, c8.shape, 0)
        return tuple(jnp.sum(jnp.where(hrow == 2 * pair + i, c8, 0.0), axis=0, keepdims=True)
                     for i in range(2))

    cq0 = tuple(r[:, 0:1] for r in crows(pl.multiple_of(q0, BLOCK), BLOCK))

    def chunk(jc, carry, causal):
        m, l, acc = carry
        ks = pl.multiple_of(jc * tk, tk)
        kt = kt_ref[0, :, pl.ds(ks, tk)]
        vv = v_ref[0, pl.ds(ks, tk), :]
        vfirst = lax.broadcasted_iota(jnp.int32, vv.shape, 1) < HEAD_DIM
        vzero = jnp.zeros_like(vv)
        vm = (jnp.where(vfirst, vv, vzero), jnp.where(vfirst, vzero, vv))
        cr = crows(ks, tk)
        m_out, l_out, alphas, pv = [], [], [], None
        for i in range(2):
            s = _dot(qm[i], kt) + (cq0[i] - cr[i]) * LOG2E
            if causal:
                s = jnp.where(ks + kcol <= qpos, s, NEG)
            m_new = jnp.maximum(m[i], jnp.max(s, axis=-1, keepdims=True))
            alpha = jnp.exp2(m[i] - m_new)
            p = jnp.exp2(s - m_new)
            m_out.append(m_new)
            l_out.append(alpha * l[i] + jnp.sum(p, axis=-1, keepdims=True))
            alphas.append(alpha)
            d = _dot(p.astype(BF16), vm[i])
            pv = d if pv is None else pv + d
        acc = jnp.where(first, alphas[0], alphas[1]) * acc + pv
        return tuple(m_out), tuple(l_out), acc

    n_full = q0 // tk
    n_chunks = (q0 + tq + tk - 1) // tk
    neg = jnp.full((tq, 1), NEG, F32)
    zero = jnp.zeros((tq, 1), F32)
    carry = ((neg, neg), (zero, zero), jnp.zeros((tq, 2 * HEAD_DIM), F32))
    carry = lax.fori_loop(0, n_full, functools.partial(chunk, causal=False), carry)
    m, l, acc = lax.fori_loop(n_full, n_chunks, functools.partial(chunk, causal=True), carry)
    o_ref[0] = (acc / jnp.where(first, l[0], l[1])).astype(BF16)


def _fox(zq, kbt, vb, c):
    B, S, _ = zq.shape
    P = vb.shape[1]
    npair = N_HEADS_B // 2
    qblk0 = W_QA // (2 * HEAD_DIM)
    return pl.pallas_call(
        _fox_kernel,
        grid=(B, npair, S // FOX_TQ),
        in_specs=[
            pl.BlockSpec((1, FOX_TQ, 2 * HEAD_DIM), lambda b, p, i: (b, i, qblk0 + p)),
            pl.BlockSpec((1, 2 * HEAD_DIM, P), lambda b, p, i: (b, p, 0)),
            pl.BlockSpec((1, P, 2 * HEAD_DIM), lambda b, p, i: (b, 0, p)),
            pl.BlockSpec((1, N_HEADS_B, P), lambda b, p, i: (b, 0, 0)),
        ],
        out_specs=pl.BlockSpec((1, FOX_TQ, 2 * HEAD_DIM), lambda b, p, i: (b, i, p)),
        out_shape=jax.ShapeDtypeStruct((B, S, W_B), BF16),
        compiler_params=pltpu.CompilerParams(
            dimension_semantics=("arbitrary", "arbitrary", "arbitrary"), vmem_limit_bytes=VMEM_LIMIT),
        name="fox",
    )(zq, kbt, vb, c)


MERGE_TM = 512


def _merge_kernel(x_ref, oa_ref, ob_ref, ga_ref, gb_ref, g0_ref, b0_ref, wpa_ref, wpb_ref, wo_ref,
                  g1_ref, b1_ref, h1_ref, h1t_ref):
    h = _ln(x_ref[...], g0_ref[...], b0_ref[...])
    ya = _dot(oa_ref[...], wpa_ref[...])
    yb = _dot(ob_ref[...], wpb_ref[...])
    merged = jax.nn.sigmoid(ga_ref[...].astype(F32)) * ya + jax.nn.sigmoid(gb_ref[...].astype(F32)) * yb
    y = _dot(merged.astype(BF16), wo_ref[...])
    h1 = _ln(DEEPNORM_ALPHA * h + y, g1_ref[...], b1_ref[...])
    h1_ref[...] = h1
    h1t_ref[...] = h1.T.astype(BF16)


def _merge(x2, oa2, ob2, zq2, g0, b0, wpa, wpb, wo, g1, b1):
    T, D = x2.shape
    tm = MERGE_TM
    full = lambda a: pl.BlockSpec(a.shape, lambda i: (0,) * a.ndim)
    gate0 = (W_QA + W_B) // D
    return pl.pallas_call(
        _merge_kernel,
        grid=(T // tm,),
        in_specs=[
            pl.BlockSpec((tm, D), lambda i: (i, 0)),
            pl.BlockSpec((tm, W_QA), lambda i: (i, 0)),
            pl.BlockSpec((tm, W_B), lambda i: (i, 0)),
            pl.BlockSpec((tm, D), lambda i: (i, gate0)),
            pl.BlockSpec((tm, D), lambda i: (i, gate0 + 1)),
            full(g0), full(b0), full(wpa), full(wpb), full(wo), full(g1), full(b1),
        ],
        out_specs=[pl.BlockSpec((tm, D), lambda i: (i, 0)), pl.BlockSpec((D, tm), lambda i: (0, i))],
        out_shape=[jax.ShapeDtypeStruct((T, D), F32), jax.ShapeDtypeStruct((D, T), BF16)],
        compiler_params=pltpu.CompilerParams(
            dimension_semantics=("arbitrary",), vmem_limit_bytes=VMEM_LIMIT),
        name="merge",
    )(x2, oa2, ob2, zq2, zq2, g0, b0, wpa, wpb, wo, g1, b1)


ROUTE_TM = 512
LANES = 128
HALF_DQ = PEER_DQ // 2
RANK_OUT = 64.0


def _top_sorted(s):
    rank = jnp.full(s.shape, RANK_OUT, F32)
    tops = []
    for r in range(PEER_TOPK):
        m = jnp.max(s, axis=0, keepdims=True)
        eq = s == m
        rank = jnp.where(eq, float(r), rank)
        s = jnp.where(eq, -jnp.inf, s)
        tops.append(m)
    return tops, rank


def _stack_rows(rows):
    n = len(rows)
    ri = lax.broadcasted_iota(jnp.int32, (n, rows[0].shape[1]), 0)
    out = jnp.broadcast_to(rows[0], ri.shape)
    for r in range(1, n):
        out = jnp.where(ri == r, rows[r], out)
    return out


def _route_kernel(ht_ref, wqt_ref, sk_ref, na_ref, e1_ref, rb_ref, e2_ref, q_ref, sc_ref):
    K = PEER_TOPK
    q_ref[...] = _dot(wqt_ref[...], ht_ref[...]).astype(BF16)
    for h in range(PEER_HEADS):
        sc_ref[h] = _dot(sk_ref[h], q_ref[h * PEER_DQ:(h + 1) * PEER_DQ, :])

    n_groups = ht_ref.shape[1] // LANES

    def group(idx, _):
        h = idx // n_groups
        g = idx % n_groups
        lanes = pl.ds(pl.multiple_of(g * LANES, LANES), LANES)
        s1 = sc_ref[h, :PEER_NKEYS, lanes]
        s2 = sc_ref[h, PEER_NKEYS:, lanes]
        a, ra = _top_sorted(s1)
        b, rb = _top_sorted(s2)
        bmat16 = _stack_rows(b)
        bmat8 = bmat16[:8]
        cands = [a[0] + bmat16]
        for i in range(1, 8):
            cands.append(a[i] + bmat8)
        cands.append(_stack_rows(a[8:]) + b[0])
        cand = jnp.concatenate(cands, axis=0)
        vmax = a[0] + b[0]
        z = jnp.zeros_like(vmax)
        for r in range(K):
            m = jnp.max(cand, axis=0, keepdims=True)
            cand = jnp.where(cand == m, -jnp.inf, cand)
            z = z + jnp.exp(m - vmax)
        sel = jnp.where(cand == -jnp.inf, 1.0, 0.0)
        counts = [jnp.sum(sel[0:16], axis=0, keepdims=True)]
        for i in range(1, 8):
            counts.append(jnp.sum(sel[8 + 8 * i:16 + 8 * i], axis=0, keepdims=True))
        n_tail = jnp.sum(sel[72:80], axis=0, keepdims=True)
        na = jnp.where(ra < 8.0 + n_tail, 1.0, 0.0)
        for i in range(8):
            na = jnp.where(ra == float(i), counts[i], na)
        na_ref[h, :, lanes] = na
        e1_ref[h, :, lanes] = jnp.exp(s1 - a[0])
        rb_ref[h, :, lanes] = rb.astype(BF16)
        e2_ref[h, :, lanes] = (jnp.exp(s2 - b[0]) * (0.5 / z)).astype(BF16)
        return 0

    lax.fori_loop(0, PEER_HEADS * n_groups, group, 0)


def _route(h1t, wqt, skbd):
    D, T = h1t.shape
    tm = ROUTE_TM
    full = lambda a: pl.BlockSpec(a.shape, lambda i: (0,) * a.ndim)
    ospec = pl.BlockSpec((PEER_HEADS, PEER_NKEYS, tm), lambda i: (0, 0, i))
    oshape = lambda dt: jax.ShapeDtypeStruct((PEER_HEADS, PEER_NKEYS, T), dt)
    return pl.pallas_call(
        _route_kernel,
        grid=(T // tm,),
        in_specs=[pl.BlockSpec((D, tm), lambda i: (0, i)), full(wqt), full(skbd)],
        out_specs=[ospec, ospec, ospec, ospec],
        out_shape=[oshape(F32), oshape(F32), oshape(BF16), oshape(BF16)],
        scratch_shapes=[pltpu.VMEM((PEER_HEADS * PEER_DQ, tm), BF16),
                        pltpu.VMEM((PEER_HEADS, 2 * PEER_NKEYS, tm), F32)],
        compiler_params=pltpu.CompilerParams(
            dimension_semantics=("arbitrary",), vmem_limit_bytes=VMEM_LIMIT),
        name="route",
    )(h1t, wqt, skbd)


EXP_TM = 512
EXP_TE = 2048
EXP_IA = EXP_TE // PEER_NKEYS
BF16_ROWS = 16
SQRT_HALF = 0.7071067811865476


def _expert_kernel(xt_ref, h1_ref, u_ref, v_ref, na_ref, e1_ref, rb_ref, e2_ref, g2_ref, b2_ref,
                   o_ref, acc_ref, hu_ref, a_ref):
    j = pl.program_id(1)
    nj = pl.num_programs(1)
    tm = xt_ref.shape[1]
    nsub = PEER_NKEYS // BF16_ROWS

    @pl.when(j == 0)
    def _():
        acc_ref[...] = jnp.zeros_like(acc_ref)

    hu_ref[...] = _dot(u_ref[...], xt_ref[...])
    rows = pl.ds(pl.multiple_of(j * EXP_IA, EXP_IA), EXP_IA)
    for il in range(EXP_IA):
        sl = slice(il * PEER_NKEYS, (il + 1) * PEER_NKEYS)
        hu = hu_ref[sl, :]
        hb = hu.astype(BF16)
        act = hb * (1.0 + lax.erf(hb * SQRT_HALF))
        gate = jnp.zeros((nsub, BF16_ROWS, tm), BF16)
        for h in range(PEER_HEADS):
            na = jnp.broadcast_to(na_ref[h, rows, :][il:il + 1], (BF16_ROWS, tm)).astype(BF16)
            e1 = jnp.broadcast_to(e1_ref[h, rows, :][il:il + 1], (BF16_ROWS, tm)).astype(BF16)
            rb = rb_ref[h].reshape(nsub, BF16_ROWS, tm)
            w = e2_ref[h].reshape(nsub, BF16_ROWS, tm) * e1[None]
            gate = gate + jnp.where(rb < na[None], w, jnp.zeros_like(w))
        a_ref[sl, :] = act * gate.reshape(PEER_NKEYS, tm)
    acc_ref[...] += _dot_tn(a_ref[...], v_ref[...])

    @pl.when(j == nj - 1)
    def _():
        o_ref[...] = _ln(DEEPNORM_ALPHA * h1_ref[...] + acc_ref[...], g2_ref[...], b2_ref[...])


def _experts(h1t, h1, u, v, na, e1, rb, e2, g2, b2):
    T, D = h1.shape
    E = u.shape[0]
    tm, te = EXP_TM, EXP_TE
    gspec = pl.BlockSpec((PEER_HEADS, PEER_NKEYS, tm), lambda i, j: (0, 0, i))
    vec = pl.BlockSpec((1, D), lambda i, j: (0, 0))
    return pl.pallas_call(
        _expert_kernel,
        grid=(T // tm, E // te),
        in_specs=[
            pl.BlockSpec((D, tm), lambda i, j: (0, i)),
            pl.BlockSpec((tm, D), lambda i, j: (i, 0)),
            pl.BlockSpec((te, D), lambda i, j: (j, 0)),
            pl.BlockSpec((te, D), lambda i, j: (j, 0)),
            gspec, gspec, gspec, gspec, vec, vec,
        ],
        out_specs=pl.BlockSpec((tm, D), lambda i, j: (i, 0)),
        out_shape=jax.ShapeDtypeStruct((T, D), F32),
        scratch_shapes=[pltpu.VMEM((tm, D), F32), pltpu.VMEM((te, tm), F32), pltpu.VMEM((te, tm), BF16)],
        compiler_params=pltpu.CompilerParams(
            dimension_semantics=("arbitrary", "arbitrary"), vmem_limit_bytes=VMEM_LIMIT),
        name="experts",
    )(h1t, h1, u, v, na, e1, rb, e2, g2, b2)


def kernel(x, meta_tokens, emb_ln_g, emb_ln_b, rel_bias, w_in, b_in, attn_sinks, w_pa, w_pb, w_o,
           ln1_g, ln1_b, peer_wq, peer_subkeys, peer_u, peer_v, ln2_g, ln2_b):
    B, S, D = x.shape
    T = B * S
    row = lambda a: a.reshape(1, -1).astype(F32)

    w, bi = w_in[0], b_in[0]
    sizes = (W_QA, W_KA, W_KA, W_B, W_B, W_B, N_HEADS_B, D, D)
    offs = np.cumsum((0,) + sizes)
    col = lambda a, k: a[..., offs[k]:offs[k + 1]]
    scale = HEAD_DIM ** -0.5 * LOG2E
    wq = jnp.concatenate([col(w, 0) * scale, col(w, 3) * scale, col(w, 7), col(w, 8)], axis=-1).astype(BF16)
    bq = jnp.concatenate([col(bi, 0) * scale, col(bi, 3) * scale, col(bi, 7), col(bi, 8)], axis=-1)
    dup = lambda a: jnp.repeat(a.reshape(a.shape[:-1] + (N_KV_A, 1, HEAD_DIM)), 2, axis=-2).reshape(
        a.shape[:-1] + (2 * W_KA,))
    wkv = jnp.concatenate([dup(col(w, 1)), dup(col(w, 2)), col(w, 5)], axis=-1).astype(BF16)
    bkv = jnp.concatenate([dup(col(bi, 1)), dup(col(bi, 2)), col(bi, 5)], axis=-1)
    wkbt = col(w, 4).T.astype(BF16)
    bkbt = col(bi, 4).reshape(-1, 1).astype(F32)
    fpad = 16 - N_HEADS_B
    wfbt = jnp.pad(col(w, 6).T, ((0, fpad), (0, 0))).astype(BF16)
    bfbt = jnp.pad(col(bi, 6), (0, fpad)).reshape(-1, 1).astype(F32)

    g0, b0 = row(emb_ln_g), row(emb_ln_b)
    kva, vb, kbt, c, zq = _inproj(x, meta_tokens.astype(F32), g0, b0, wkv, row(bkv), wkbt, bkbt,
                                  wfbt, bfbt, wq, row(bq))
    oa = _swa(zq, kva, rel_bias.astype(F32), attn_sinks[0].astype(F32))
    ob = _fox(zq, kbt, vb, c)

    h1, h1t = _merge(x.reshape(T, D), oa.reshape(T, W_QA), ob.reshape(T, W_B), zq.reshape(T, -1),
                     g0, b0, w_pa[0].astype(BF16), w_pb[0].astype(BF16), w_o[0].astype(BF16),
                     row(ln1_g[0]), row(ln1_b[0]))

    wqt = peer_wq[0].T.astype(BF16)
    sk = peer_subkeys[0].astype(BF16)
    zero = jnp.zeros_like(sk[:, 0])
    skbd = jnp.concatenate([jnp.concatenate([sk[:, 0], zero], axis=-1),
                            jnp.concatenate([zero, sk[:, 1]], axis=-1)], axis=1)
    na, e1, rb, e2 = _route(h1t, wqt, skbd)
    out = _experts(h1t, h1, peer_u[0].astype(BF16), peer_v[0].astype(BF16), na, e1, rb, e2,
                   row(ln2_g[0]), row(ln2_b[0]))
    return out.reshape(B, S, D)
```

```python
import functools
import math

import jax
import jax.numpy as jnp
import numpy as np
from jax import lax
from jax.experimental import pallas as pl
from jax.experimental.pallas import tpu as pltpu

N_META = 16
BLOCK = 128
PAD = BLOCK - N_META
HEAD_DIM = 64
N_HEADS_A = 8
N_KV_A = 2
GROUP_A = N_HEADS_A // N_KV_A
WINDOW = 128
N_HEADS_B = 8
T5_BUCKETS = 32
T5_MAX_DIST = 128
PEER_HEADS = 8
PEER_NKEYS = 128
PEER_DQ = 256
PEER_TOPK = 16
LN_EPS = 1e-5
NEG = -1e30
LOG2E = 1.4426950408889634
DEPTH = 1
DEEPNORM_ALPHA = (2 * DEPTH) ** 0.25

W_QA = N_HEADS_A * HEAD_DIM
W_KA = N_KV_A * HEAD_DIM
W_B = N_HEADS_B * HEAD_DIM
KVA_W = 4 * W_KA

F32 = jnp.float32
BF16 = jnp.bfloat16

VMEM_LIMIT = 56 * 1024 * 1024


def _ln(x, g, b):
    mu = jnp.mean(x, axis=-1, keepdims=True)
    xc = x - mu
    var = jnp.mean(xc * xc, axis=-1, keepdims=True)
    return xc * lax.rsqrt(var + LN_EPS) * g + b


def _dot(a, b):
    return jnp.dot(a, b, preferred_element_type=F32)


def _dot_nt(a, b):
    return lax.dot_general(a, b, (((1,), (1,)), ((), ())), preferred_element_type=F32)


def _dot_tn(a, b):
    return lax.dot_general(a, b, (((0,), (0,)), ((), ())), preferred_element_type=F32)


def _split3(x):
    hi = x.astype(BF16)
    r = x - hi.astype(F32)
    mid = r.astype(BF16)
    lo = (r - mid.astype(F32)).astype(BF16)
    return hi, mid, lo


QT_COLS = 512


def _inproj_kernel(x_ref, meta_ref, g0_ref, b0_ref, wkv_ref, bkv_ref, wkbt_ref, bkbt_ref,
                   wfbt_ref, bfbt_ref, wq_ref, bq_ref,
                   kva_ref, vb_ref, kbt_ref, c_ref, zq_ref, hs_ref):
    j = pl.program_id(1)
    P = hs_ref.shape[0]

    @pl.when(j == 0)
    def _():
        g0 = g0_ref[...]
        b0 = b0_ref[...]
        hs_ref[0:PAD, :] = jnp.zeros((PAD, hs_ref.shape[1]), BF16)
        hs_ref[PAD:BLOCK, :] = _ln(meta_ref[...], g0, b0).astype(BF16)
        hs_ref[BLOCK:, :] = _ln(x_ref[0], g0, b0).astype(BF16)
        hs = hs_ref[...]
        nkv = KVA_W
        kva_ref[0] = (_dot(hs, wkv_ref[:, :nkv]) + bkv_ref[:, :nkv]).astype(BF16)
        vb_ref[0, :P, :] = (_dot(hs, wkv_ref[:, nkv:]) + bkv_ref[:, nkv:]).astype(BF16)
        vb_ref[0, P:, :] = jnp.zeros((vb_ref.shape[1] - P, vb_ref.shape[2]), BF16)
        kbt_ref[0, :, :P] = (_dot_nt(wkbt_ref[...], hs) + bkbt_ref[...]).astype(BF16)
        kbt_ref[0, :, P:] = jnp.zeros((kbt_ref.shape[1], kbt_ref.shape[2] - P), BF16)
        c_ref[0, :, P:] = jnp.full((c_ref.shape[1], c_ref.shape[2] - P), C_MASKED, F32)
        fbt = _dot_nt(wfbt_ref[...], hs) + bfbt_ref[...]
        lf = jax.nn.log_sigmoid(fbt)
        pos = lax.broadcasted_iota(jnp.int32, lf.shape, 1)
        lf = jnp.where(pos >= PAD, lf, 0.0)
        ri = lax.broadcasted_iota(jnp.int32, (BLOCK, BLOCK), 0)
        ci = lax.broadcasted_iota(jnp.int32, (BLOCK, BLOCK), 1)
        tri = jnp.where(ri <= ci, 1.0, 0.0).astype(BF16)
        carry = jnp.zeros((lf.shape[0], 1), F32)
        for n in range(P // BLOCK):
            hi, mid, lo = _split3(lf[:, n * BLOCK:(n + 1) * BLOCK])
            cb = _dot(hi, tri) + _dot(mid, tri) + _dot(lo, tri) + carry
            carry = cb[:, BLOCK - 1:BLOCK]
            if n == 0:
                cb = jnp.where(lax.broadcasted_iota(jnp.int32, cb.shape, 1) >= PAD, cb, C_MASKED)
            c_ref[0, :, n * BLOCK:(n + 1) * BLOCK] = cb[:N_HEADS_B]

    @pl.when(j > 0)
    def _():
        zq_ref[0] = (_dot(hs_ref[BLOCK:, :], wq_ref[...]) + bq_ref[...]).astype(BF16)


def _inproj(x, meta, g0, b0, wkv, bkv, wkbt, bkbt, wfbt, bfbt, wq, bq):
    B, S, D = x.shape
    P = S + BLOCK
    P2 = -(-P // FOX_TK) * FOX_TK
    nq = wq.shape[1] // QT_COLS
    full = lambda shape: pl.BlockSpec(shape, lambda b, j: (0,) * len(shape))
    qcol = lambda b, j: (0, jnp.maximum(j - 1, 0))
    return pl.pallas_call(
        _inproj_kernel,
        grid=(B, nq + 1),
        in_specs=[
            pl.BlockSpec((1, S, D), lambda b, j: (b, 0, 0)),
            full(meta.shape), full(g0.shape), full(b0.shape),
            full(wkv.shape), full(bkv.shape), full(wkbt.shape), full(bkbt.shape),
            full(wfbt.shape), full(bfbt.shape),
            pl.BlockSpec((D, QT_COLS), qcol),
            pl.BlockSpec((1, QT_COLS), qcol),
        ],
        out_specs=[
            pl.BlockSpec((1, P, KVA_W), lambda b, j: (b, 0, 0)),
            pl.BlockSpec((1, P2, W_B), lambda b, j: (b, 0, 0)),
            pl.BlockSpec((1, W_B, P2), lambda b, j: (b, 0, 0)),
            pl.BlockSpec((1, N_HEADS_B, P2), lambda b, j: (b, 0, 0)),
            pl.BlockSpec((1, S, QT_COLS), lambda b, j: (b, 0, jnp.maximum(j - 1, 0))),
        ],
        out_shape=[
            jax.ShapeDtypeStruct((B, P, KVA_W), BF16),
            jax.ShapeDtypeStruct((B, P2, W_B), BF16),
            jax.ShapeDtypeStruct((B, W_B, P2), BF16),
            jax.ShapeDtypeStruct((B, N_HEADS_B, P2), F32),
            jax.ShapeDtypeStruct((B, S, wq.shape[1]), BF16),
        ],
        scratch_shapes=[pltpu.VMEM((P, D), BF16)],
        compiler_params=pltpu.CompilerParams(
            dimension_semantics=("arbitrary", "arbitrary"), vmem_limit_bytes=VMEM_LIMIT),
        name="inproj",
    )(x, meta, g0, b0, wkv, bkv, wkbt, bkbt, wfbt, bfbt, wq, bq)


def _t5_bucket_table():
    qi = np.arange(BLOCK)[:, None] + BLOCK
    kj = np.arange(2 * BLOCK)[None, :]
    n = np.maximum(qi - kj, 0)
    max_exact = T5_BUCKETS // 2
    nf = np.maximum(n, 1).astype(np.float32)
    large = max_exact + (np.log(nf / np.float32(max_exact)) / np.float32(math.log(T5_MAX_DIST / max_exact))
                         * (T5_BUCKETS - max_exact)).astype(np.int32)
    large = np.minimum(large, T5_BUCKETS - 1)
    bucket = np.where(n < max_exact, n, large).astype(np.int32)
    dist = qi - kj
    window = (dist >= 0) & (dist < WINDOW)
    return np.where(window, bucket, -1).astype(np.int32)


def _swa_kernel(bucket_ref, rel_ref, sink_ref, q_ref, kvp_ref, kvc_ref, o_ref, bias_ref):
    b = pl.program_id(0)
    n = pl.program_id(1)

    @pl.when((b == 0) & (n == 0))
    def _():
        bucket = bucket_ref[...]
        for h in range(N_HEADS_A):
            acc = jnp.where(bucket < 0, NEG, 0.0).astype(F32)
            for t in range(T5_BUCKETS):
                acc = jnp.where(bucket == t, rel_ref[t, h] * LOG2E, acc)
            bias_ref[h] = acc

    kcol = lax.broadcasted_iota(jnp.int32, (BLOCK, 2 * BLOCK), 1)
    kabs_ok = (n * BLOCK + kcol) >= PAD
    pw = 2 * HEAD_DIM
    first = lax.broadcasted_iota(jnp.int32, (BLOCK, pw), 1) < HEAD_DIM
    vfirst = lax.broadcasted_iota(jnp.int32, (2 * BLOCK, pw), 1) < HEAD_DIM
    for g in range(N_KV_A):
        kk = jnp.concatenate([kvp_ref[0, :, g * pw:(g + 1) * pw],
                              kvc_ref[0, :, g * pw:(g + 1) * pw]], axis=0)
        vv = jnp.concatenate([kvp_ref[0, :, (N_KV_A + g) * pw:(N_KV_A + g + 1) * pw],
                              kvc_ref[0, :, (N_KV_A + g) * pw:(N_KV_A + g + 1) * pw]], axis=0)
        vzero = jnp.zeros_like(vv)
        vm = (jnp.where(vfirst, vv, vzero), jnp.where(vfirst, vzero, vv))
        for pp in range(GROUP_A // 2):
            pair = g * (GROUP_A // 2) + pp
            q = q_ref[0, :, pair * pw:(pair + 1) * pw]
            qzero = jnp.zeros_like(q)
            qm = (jnp.where(first, q, qzero), jnp.where(first, qzero, q))
            out = None
            for i in range(2):
                h = 2 * pair + i
                s = _dot_nt(qm[i], kk) + bias_ref[h]
                s = jnp.where(kabs_ok, s, NEG)
                sink = sink_ref[h] * LOG2E
                m = jnp.maximum(jnp.max(s, axis=-1, keepdims=True), sink)
                p = jnp.exp2(s - m)
                denom = jnp.sum(p, axis=-1, keepdims=True) + jnp.exp2(sink - m)
                o = _dot(p.astype(BF16), vm[i]) / denom
                out = o if out is None else out + o
            o_ref[0, :, pair * pw:(pair + 1) * pw] = out.astype(BF16)


def _swa(zq, kva, rel_bias, sinks):
    B, S, _ = zq.shape
    nb = S // BLOCK
    bucket = jnp.asarray(_t5_bucket_table())
    smem = pl.BlockSpec(memory_space=pltpu.SMEM)
    return pl.pallas_call(
        _swa_kernel,
        grid=(B, nb),
        in_specs=[
            pl.BlockSpec((BLOCK, 2 * BLOCK), lambda b, n: (0, 0)),
            smem, smem,
            pl.BlockSpec((1, BLOCK, W_QA), lambda b, n: (b, n, 0)),
            pl.BlockSpec((1, BLOCK, KVA_W), lambda b, n: (b, n, 0)),
            pl.BlockSpec((1, BLOCK, KVA_W), lambda b, n: (b, n + 1, 0)),
        ],
        out_specs=pl.BlockSpec((1, BLOCK, W_QA), lambda b, n: (b, n, 0)),
        out_shape=jax.ShapeDtypeStruct((B, S, W_QA), BF16),
        scratch_shapes=[pltpu.VMEM((N_HEADS_A, BLOCK, 2 * BLOCK), F32)],
        compiler_params=pltpu.CompilerParams(
            dimension_semantics=("arbitrary", "arbitrary"), vmem_limit_bytes=VMEM_LIMIT),
        name="swa",
    )(bucket, rel_bias, sinks, zq, kva, kva)


FOX_TQ = 256
FOX_TK = 512
C_MASKED = 1e30


def _fox_kernel(q_ref, kt_ref, v_ref, c_ref, o_ref):
    pair = pl.program_id(1)
    qi = pl.program_id(2)
    tq = q_ref.shape[1]
    tk = FOX_TK
    q0 = BLOCK + qi * tq
    q = q_ref[0]
    first = lax.broadcasted_iota(jnp.int32, q.shape, 1) < HEAD_DIM
    qzero = jnp.zeros_like(q)
    qm = (jnp.where(first, q, qzero), jnp.where(first, qzero, q))
    qpos = q0 + lax.broadcasted_iota(jnp.int32, (tq, tk), 0)
    kcol = lax.broadcasted_iota(jnp.int32, (tq, tk), 1)

    def crows(start, size):
        c8 = c_ref[0, :, pl.ds(start, size)]
        hrow = lax.broadcasted_iota(jnp.int32, c8.shape, 0)
        return tuple(jnp.sum(jnp.where(hrow == 2 * pair + i, c8, 0.0), axis=0, keepdims=True)
                     for i in range(2))

    cq0 = tuple(r[:, 0:1] for r in crows(pl.multiple_of(q0, BLOCK), BLOCK))

    def chunk(jc, carry, causal):
        m, l, acc = carry
        ks = pl.multiple_of(jc * tk, tk)
        kt = kt_ref[0, :, pl.ds(ks, tk)]
        vv = v_ref[0, pl.ds(ks, tk), :]
        vfirst = lax.broadcasted_iota(jnp.int32, vv.shape, 1) < HEAD_DIM
        vzero = jnp.zeros_like(vv)
        vm = (jnp.where(vfirst, vv, vzero), jnp.where(vfirst, vzero, vv))
        cr = crows(ks, tk)
        m_out, l_out, alphas, pv = [], [], [], None
        for i in range(2):
            s = _dot(qm[i], kt) + (cq0[i] - cr[i]) * LOG2E
            if causal:
                s = jnp.where(ks + kcol <= qpos, s, NEG)
            m_new = jnp.maximum(m[i], jnp.max(s, axis=-1, keepdims=True))
            alpha = jnp.exp2(m[i] - m_new)
            p = jnp.exp2(s - m_new)
            m_out.append(m_new)
            l_out.append(alpha * l[i] + jnp.sum(p, axis=-1, keepdims=True))
            alphas.append(alpha)
            d = _dot(p.astype(BF16), vm[i])
            pv = d if pv is None else pv + d
        acc = jnp.where(first, alphas[0], alphas[1]) * acc + pv
        return tuple(m_out), tuple(l_out), acc

    n_full = q0 // tk
    n_chunks = (q0 + tq + tk - 1) // tk
    neg = jnp.full((tq, 1), NEG, F32)
    zero = jnp.zeros((tq, 1), F32)
    carry = ((neg, neg), (zero, zero), jnp.zeros((tq, 2 * HEAD_DIM), F32))
    carry = lax.fori_loop(0, n_full, functools.partial(chunk, causal=False), carry)
    m, l, acc = lax.fori_loop(n_full, n_chunks, functools.partial(chunk, causal=True), carry)
    o_ref[0] = (acc / jnp.where(first, l[0], l[1])).astype(BF16)


def _fox(zq, kbt, vb, c):
    B, S, _ = zq.shape
    P = vb.shape[1]
    npair = N_HEADS_B // 2
    qblk0 = W_QA // (2 * HEAD_DIM)
    return pl.pallas_call(
        _fox_kernel,
        grid=(B, npair, S // FOX_TQ),
        in_specs=[
            pl.BlockSpec((1, FOX_TQ, 2 * HEAD_DIM), lambda b, p, i: (b, i, qblk0 + p)),
            pl.BlockSpec((1, 2 * HEAD_DIM, P), lambda b, p, i: (b, p, 0)),
            pl.BlockSpec((1, P, 2 * HEAD_DIM), lambda b, p, i: (b, 0, p)),
            pl.BlockSpec((1, N_HEADS_B, P), lambda b, p, i: (b, 0, 0)),
        ],
        out_specs=pl.BlockSpec((1, FOX_TQ, 2 * HEAD_DIM), lambda b, p, i: (b, i, p)),
        out_shape=jax.ShapeDtypeStruct((B, S, W_B), BF16),
        compiler_params=pltpu.CompilerParams(
            dimension_semantics=("arbitrary", "arbitrary", "arbitrary"), vmem_limit_bytes=VMEM_LIMIT),
        name="fox",
    )(zq, kbt, vb, c)


MERGE_TM = 512


def _merge_kernel(x_ref, oa_ref, ob_ref, ga_ref, gb_ref, g0_ref, b0_ref, wpa_ref, wpb_ref, wo_ref,
                  g1_ref, b1_ref, h1_ref, h1t_ref):
    h = _ln(x_ref[...], g0_ref[...], b0_ref[...])
    ya = _dot(oa_ref[...], wpa_ref[...])
    yb = _dot(ob_ref[...], wpb_ref[...])
    merged = jax.nn.sigmoid(ga_ref[...].astype(F32)) * ya + jax.nn.sigmoid(gb_ref[...].astype(F32)) * yb
    y = _dot(merged.astype(BF16), wo_ref[...])
    h1 = _ln(DEEPNORM_ALPHA * h + y, g1_ref[...], b1_ref[...])
    h1_ref[...] = h1
    h1t_ref[...] = h1.T.astype(BF16)


def _merge(x2, oa2, ob2, zq2, g0, b0, wpa, wpb, wo, g1, b1):
    T, D = x2.shape
    tm = MERGE_TM
    full = lambda a: pl.BlockSpec(a.shape, lambda i: (0,) * a.ndim)
    gate0 = (W_QA + W_B) // D
    return pl.pallas_call(
        _merge_kernel,
        grid=(T // tm,),
        in_specs=[
            pl.BlockSpec((tm, D), lambda i: (i, 0)),
            pl.BlockSpec((tm, W_QA), lambda i: (i, 0)),
            pl.BlockSpec((tm, W_B), lambda i: (i, 0)),
            pl.BlockSpec((tm, D), lambda i: (i, gate0)),
            pl.BlockSpec((tm, D), lambda i: (i, gate0 + 1)),
            full(g0), full(b0), full(wpa), full(wpb), full(wo), full(g1), full(b1),
        ],
        out_specs=[pl.BlockSpec((tm, D), lambda i: (i, 0)), pl.BlockSpec((D, tm), lambda i: (0, i))],
        out_shape=[jax.ShapeDtypeStruct((T, D), F32), jax.ShapeDtypeStruct((D, T), BF16)],
        compiler_params=pltpu.CompilerParams(
            dimension_semantics=("arbitrary",), vmem_limit_bytes=VMEM_LIMIT),
        name="merge",
    )(x2, oa2, ob2, zq2, zq2, g0, b0, wpa, wpb, wo, g1, b1)


ROUTE_TM = 512
ROUTE_UNROLL = 2
LANES = 128
HALF_DQ = PEER_DQ // 2
RANK_OUT = 64.0


def _top_sorted(s, with_rank):
    rank = jnp.full(s.shape, RANK_OUT, F32) if with_rank else None
    tops = []
    for r in range(PEER_TOPK):
        m = jnp.max(s, axis=0, keepdims=True)
        eq = s == m
        if with_rank:
            rank = jnp.where(eq, float(r), rank)
        s = jnp.where(eq, -jnp.inf, s)
        tops.append(m)
    return tops, rank


def _stack_rows(rows):
    n = len(rows)
    ri = lax.broadcasted_iota(jnp.int32, (n, rows[0].shape[1]), 0)
    out = jnp.broadcast_to(rows[0], ri.shape)
    for r in range(1, n):
        out = jnp.where(ri == r, rows[r], out)
    return out


def _route_kernel(ht_ref, wqt_ref, sk_ref, na_ref, e1_ref, rb_ref, e2_ref, q_ref, sc_ref):
    K = PEER_TOPK
    q_ref[...] = _dot(wqt_ref[...], ht_ref[...]).astype(BF16)
    for h in range(PEER_HEADS):
        sc_ref[h] = _dot(sk_ref[h], q_ref[h * PEER_DQ:(h + 1) * PEER_DQ, :])

    n_groups = ht_ref.shape[1] // LANES

    def one_group(h, g):
        lanes = pl.ds(pl.multiple_of(g * LANES, LANES), LANES)
        s1 = sc_ref[h, :PEER_NKEYS, lanes]
        s2 = sc_ref[h, PEER_NKEYS:, lanes]
        a, _ = _top_sorted(s1, with_rank=False)
        b, rb = _top_sorted(s2, with_rank=True)
        bmat16 = _stack_rows(b)
        bmat8 = bmat16[:8]
        cands = [a[0] + bmat16]
        for i in range(1, 8):
            cands.append(a[i] + bmat8)
        cands.append(_stack_rows(a[8:]) + b[0])
        cand = jnp.concatenate(cands, axis=0)
        vmax = a[0] + b[0]
        z = jnp.zeros_like(vmax)
        for r in range(K):
            m = jnp.max(cand, axis=0, keepdims=True)
            cand = jnp.where(cand == m, -jnp.inf, cand)
            z = z + jnp.exp(m - vmax)
        sel = jnp.where(cand == -jnp.inf, 1.0, 0.0)
        counts = [jnp.sum(sel[0:16], axis=0, keepdims=True)]
        for i in range(1, 8):
            counts.append(jnp.sum(sel[8 + 8 * i:16 + 8 * i], axis=0, keepdims=True))
        n_tail = jnp.sum(sel[72:80], axis=0, keepdims=True)
        low = jnp.full_like(n_tail, jnp.inf)
        for k in range(1, 9):
            low = jnp.where(n_tail == float(k), a[7 + k], low)
        na = jnp.where(s1 >= low, jnp.where(s1 < a[7], 1.0, 0.0), 0.0)
        for i in range(8):
            na = jnp.where(s1 == a[i], counts[i], na)
        na_ref[h, :, lanes] = na
        e1_ref[h, :, lanes] = jnp.exp(s1 - a[0])
        rb_ref[h, :, lanes] = rb.astype(BF16)
        e2_ref[h, :, lanes] = (jnp.exp(s2 - b[0]) * (0.5 / z)).astype(BF16)

    per_head = n_groups // ROUTE_UNROLL

    def group(idx, _):
        h = idx // per_head
        g0 = (idx % per_head) * ROUTE_UNROLL
        for k in range(ROUTE_UNROLL):
            one_group(h, g0 + k)
        return 0

    lax.fori_loop(0, PEER_HEADS * per_head, group, 0)


def _route(h1t, wqt, skbd):
    D, T = h1t.shape
    tm = ROUTE_TM
    full = lambda a: pl.BlockSpec(a.shape, lambda i: (0,) * a.ndim)
    ospec = pl.BlockSpec((PEER_HEADS, PEER_NKEYS, tm), lambda i: (0, 0, i))
    oshape = lambda dt: jax.ShapeDtypeStruct((PEER_HEADS, PEER_NKEYS, T), dt)
    return pl.pallas_call(
        _route_kernel,
        grid=(T // tm,),
        in_specs=[pl.BlockSpec((D, tm), lambda i: (0, i)), full(wqt), full(skbd)],
        out_specs=[ospec, ospec, ospec, ospec],
        out_shape=[oshape(F32), oshape(F32), oshape(BF16), oshape(BF16)],
        scratch_shapes=[pltpu.VMEM((PEER_HEADS * PEER_DQ, tm), BF16),
                        pltpu.VMEM((PEER_HEADS, 2 * PEER_NKEYS, tm), F32)],
        compiler_params=pltpu.CompilerParams(
            dimension_semantics=("arbitrary",), vmem_limit_bytes=VMEM_LIMIT),
        name="route",
    )(h1t, wqt, skbd)


EXP_TM = 512
EXP_TE = 2048
EXP_IA = EXP_TE // PEER_NKEYS
BF16_ROWS = 16
SQRT_HALF = 0.7071067811865476


def _expert_kernel(xt_ref, h1_ref, u_ref, v_ref, na_ref, e1_ref, rb_ref, e2_ref, g2_ref, b2_ref,
                   o_ref, acc_ref, hu_ref, a_ref):
    j = pl.program_id(1)
    nj = pl.num_programs(1)
    tm = xt_ref.shape[1]
    nsub = PEER_NKEYS // BF16_ROWS

    @pl.when(j == 0)
    def _():
        acc_ref[...] = jnp.zeros_like(acc_ref)

    hu_ref[...] = _dot(u_ref[...], xt_ref[...])
    rows = pl.ds(pl.multiple_of(j * EXP_IA, EXP_IA), EXP_IA)
    for il in range(EXP_IA):
        sl = slice(il * PEER_NKEYS, (il + 1) * PEER_NKEYS)
        hu = hu_ref[sl, :]
        hb = hu.astype(BF16)
        act = hb * (1.0 + lax.erf(hb * SQRT_HALF))
        gate = jnp.zeros((nsub, BF16_ROWS, tm), BF16)
        for h in range(PEER_HEADS):
            na = jnp.broadcast_to(na_ref[h, rows, :][il:il + 1], (BF16_ROWS, tm)).astype(BF16)
            e1 = jnp.broadcast_to(e1_ref[h, rows, :][il:il + 1], (BF16_ROWS, tm)).astype(BF16)
            rb = rb_ref[h].reshape(nsub, BF16_ROWS, tm)
            w = e2_ref[h].reshape(nsub, BF16_ROWS, tm) * e1[None]
            gate = gate + jnp.where(rb < na[None], w, jnp.zeros_like(w))
        a_ref[sl, :] = act * gate.reshape(PEER_NKEYS, tm)
    acc_ref[...] += _dot_tn(a_ref[...], v_ref[...])

    @pl.when(j == nj - 1)
    def _():
        o_ref[...] = _ln(DEEPNORM_ALPHA * h1_ref[...] + acc_ref[...], g2_ref[...], b2_ref[...])


def _experts(h1t, h1, u, v, na, e1, rb, e2, g2, b2):
    T, D = h1.shape
    E = u.shape[0]
    tm, te = EXP_TM, EXP_TE
    gspec = pl.BlockSpec((PEER_HEADS, PEER_NKEYS, tm), lambda i, j: (0, 0, i))
    vec = pl.BlockSpec((1, D), lambda i, j: (0, 0))
    return pl.pallas_call(
        _expert_kernel,
        grid=(T // tm, E // te),
        in_specs=[
            pl.BlockSpec((D, tm), lambda i, j: (0, i)),
            pl.BlockSpec((tm, D), lambda i, j: (i, 0)),
            pl.BlockSpec((te, D), lambda i, j: (j, 0)),
            pl.BlockSpec((te, D), lambda i, j: (j, 0)),
            gspec, gspec, gspec, gspec, vec, vec,
        ],
        out_specs=pl.BlockSpec((tm, D), lambda i, j: (i, 0)),
        out_shape=jax.ShapeDtypeStruct((T, D), F32),
        scratch_shapes=[pltpu.VMEM((tm, D), F32), pltpu.VMEM((te, tm), F32), pltpu.VMEM((te, tm), BF16)],
        compiler_params=pltpu.CompilerParams(
            dimension_semantics=("arbitrary", "arbitrary"), vmem_limit_bytes=VMEM_LIMIT),
        name="experts",
    )(h1t, h1, u, v, na, e1, rb, e2, g2, b2)


def kernel(x, meta_tokens, emb_ln_g, emb_ln_b, rel_bias, w_in, b_in, attn_sinks, w_pa, w_pb, w_o,
           ln1_g, ln1_b, peer_wq, peer_subkeys, peer_u, peer_v, ln2_g, ln2_b):
    B, S, D = x.shape
    T = B * S
    row = lambda a: a.reshape(1, -1).astype(F32)

    w, bi = w_in[0], b_in[0]
    sizes = (W_QA, W_KA, W_KA, W_B, W_B, W_B, N_HEADS_B, D, D)
    offs = np.cumsum((0,) + sizes)
    col = lambda a, k: a[..., offs[k]:offs[k + 1]]
    scale = HEAD_DIM ** -0.5 * LOG2E
    wq = jnp.concatenate([col(w, 0) * scale, col(w, 3) * scale, col(w, 7), col(w, 8)], axis=-1).astype(BF16)
    bq = jnp.concatenate([col(bi, 0) * scale, col(bi, 3) * scale, col(bi, 7), col(bi, 8)], axis=-1)
    dup = lambda a: jnp.repeat(a.reshape(a.shape[:-1] + (N_KV_A, 1, HEAD_DIM)), 2, axis=-2).reshape(
        a.shape[:-1] + (2 * W_KA,))
    wkv = jnp.concatenate([dup(col(w, 1)), dup(col(w, 2)), col(w, 5)], axis=-1).astype(BF16)
    bkv = jnp.concatenate([dup(col(bi, 1)), dup(col(bi, 2)), col(bi, 5)], axis=-1)
    wkbt = col(w, 4).T.astype(BF16)
    bkbt = col(bi, 4).reshape(-1, 1).astype(F32)
    fpad = 16 - N_HEADS_B
    wfbt = jnp.pad(col(w, 6).T, ((0, fpad), (0, 0))).astype(BF16)
    bfbt = jnp.pad(col(bi, 6), (0, fpad)).reshape(-1, 1).astype(F32)

    g0, b0 = row(emb_ln_g), row(emb_ln_b)
    kva, vb, kbt, c, zq = _inproj(x, meta_tokens.astype(F32), g0, b0, wkv, row(bkv), wkbt, bkbt,
                                  wfbt, bfbt, wq, row(bq))
    oa = _swa(zq, kva, rel_bias.astype(F32), attn_sinks[0].astype(F32))
    ob = _fox(zq, kbt, vb, c)

    h1, h1t = _merge(x.reshape(T, D), oa.reshape(T, W_QA), ob.reshape(T, W_B), zq.reshape(T, -1),
                     g0, b0, w_pa[0].astype(BF16), w_pb[0].astype(BF16), w_o[0].astype(BF16),
                     row(ln1_g[0]), row(ln1_b[0]))

    wqt = peer_wq[0].T.astype(BF16)
    sk = peer_subkeys[0].astype(BF16)
    zero = jnp.zeros_like(sk[:, 0])
    skbd = jnp.concatenate([jnp.concatenate([sk[:, 0], zero], axis=-1),
                            jnp.concatenate([zero, sk[:, 1]], axis=-1)], axis=1)
    na, e1, rb, e2 = _route(h1t, wqt, skbd)
    out = _experts(h1t, h1, peer_u[0].astype(BF16), peer_v[0].astype(BF16), na, e1, rb, e2,
                   row(ln2_g[0]), row(ln2_b[0]))
    return out.reshape(B, S, D)
```

```python
import functools
import math

import jax
import jax.numpy as jnp
import numpy as np
from jax import lax
from jax.experimental import pallas as pl
from jax.experimental.pallas import tpu as pltpu

N_META = 16
BLOCK = 128
PAD = BLOCK - N_META
HEAD_DIM = 64
N_HEADS_A = 8
N_KV_A = 2
GROUP_A = N_HEADS_A // N_KV_A
WINDOW = 128
N_HEADS_B = 8
T5_BUCKETS = 32
T5_MAX_DIST = 128
PEER_HEADS = 8
PEER_NKEYS = 128
PEER_DQ = 256
PEER_TOPK = 16
LN_EPS = 1e-5
NEG = -1e30
LOG2E = 1.4426950408889634
DEPTH = 1
DEEPNORM_ALPHA = (2 * DEPTH) ** 0.25

W_QA = N_HEADS_A * HEAD_DIM
W_KA = N_KV_A * HEAD_DIM
W_B = N_HEADS_B * HEAD_DIM
KVA_W = 4 * W_KA

F32 = jnp.float32
BF16 = jnp.bfloat16

VMEM_LIMIT = 56 * 1024 * 1024
VMEM_LIMIT_INPROJ = 60 * 1024 * 1024


def _ln(x, g, b):
    mu = jnp.mean(x, axis=-1, keepdims=True)
    xc = x - mu
    var = jnp.mean(xc * xc, axis=-1, keepdims=True)
    return xc * lax.rsqrt(var + LN_EPS) * g + b


def _dot(a, b):
    return jnp.dot(a, b, preferred_element_type=F32)


def _dot_nt(a, b):
    return lax.dot_general(a, b, (((1,), (1,)), ((), ())), preferred_element_type=F32)


def _dot_tn(a, b):
    return lax.dot_general(a, b, (((0,), (0,)), ((), ())), preferred_element_type=F32)


def _split3(x):
    hi = x.astype(BF16)
    r = x - hi.astype(F32)
    mid = r.astype(BF16)
    lo = (r - mid.astype(F32)).astype(BF16)
    return hi, mid, lo


QT_COLS = 512


def _inproj_kernel(x_ref, meta_ref, g0_ref, b0_ref, wkv_ref, bkv_ref, wvbt_ref, bvbt_ref,
                   wfb_ref, bfb_ref, wq_ref, bq_ref,
                   kva_ref, kb_ref, vbt_ref, c_ref, zq_ref, hs_ref):
    j = pl.program_id(1)
    P = hs_ref.shape[0]

    @pl.when(j == 0)
    def _():
        g0 = g0_ref[...]
        b0 = b0_ref[...]
        hs_ref[0:PAD, :] = jnp.zeros((PAD, hs_ref.shape[1]), BF16)
        hs_ref[PAD:BLOCK, :] = _ln(meta_ref[...], g0, b0).astype(BF16)
        hs_ref[BLOCK:, :] = _ln(x_ref[0], g0, b0).astype(BF16)
        hs = hs_ref[...]
        nkv = KVA_W
        kva_ref[0] = (_dot(hs, wkv_ref[:, :nkv]) + bkv_ref[:, :nkv]).astype(BF16)
        kb_ref[0, :P, :] = (_dot(hs, wkv_ref[:, nkv:]) + bkv_ref[:, nkv:]).astype(BF16)
        kb_ref[0, P:, :] = jnp.zeros((kb_ref.shape[1] - P, kb_ref.shape[2]), BF16)
        vbt_ref[0, :, :P] = (_dot_nt(wvbt_ref[...], hs) + bvbt_ref[...]).astype(BF16)
        vbt_ref[0, :, P:] = jnp.zeros((vbt_ref.shape[1], vbt_ref.shape[2] - P), BF16)
        c_ref[0, P:, :] = jnp.full((c_ref.shape[1] - P, c_ref.shape[2]), C_MASKED, F32)
        lf = jax.nn.log_sigmoid(_dot(hs, wfb_ref[...]) + bfb_ref[...])
        pos = lax.broadcasted_iota(jnp.int32, lf.shape, 0)
        lf = jnp.where(pos >= PAD, lf, 0.0)
        ri = lax.broadcasted_iota(jnp.int32, (BLOCK, BLOCK), 0)
        ci = lax.broadcasted_iota(jnp.int32, (BLOCK, BLOCK), 1)
        tri = jnp.where(ci <= ri, 1.0, 0.0).astype(BF16)
        carry = jnp.zeros((1, lf.shape[1]), F32)
        for n in range(P // BLOCK):
            hi, mid, lo = _split3(lf[n * BLOCK:(n + 1) * BLOCK, :])
            cb = _dot(tri, hi) + _dot(tri, mid) + _dot(tri, lo) + carry
            carry = cb[BLOCK - 1:BLOCK, :]
            if n == 0:
                cb = jnp.where(lax.broadcasted_iota(jnp.int32, cb.shape, 0) >= PAD, cb, C_MASKED)
            c_ref[0, n * BLOCK:(n + 1) * BLOCK, :] = cb

    @pl.when(j > 0)
    def _():
        zq_ref[0] = (_dot(hs_ref[BLOCK:, :], wq_ref[...]) + bq_ref[...]).astype(BF16)


def _inproj(x, meta, g0, b0, wkv, bkv, wvbt, bvbt, wfb, bfb, wq, bq):
    B, S, D = x.shape
    P = S + BLOCK
    P2 = -(-P // FOX_TK) * FOX_TK
    nq = wq.shape[1] // QT_COLS
    full = lambda shape: pl.BlockSpec(shape, lambda b, j: (0,) * len(shape))
    qcol = lambda b, j: (0, jnp.maximum(j - 1, 0))
    return pl.pallas_call(
        _inproj_kernel,
        grid=(B, nq + 1),
        in_specs=[
            pl.BlockSpec((1, S, D), lambda b, j: (b, 0, 0)),
            full(meta.shape), full(g0.shape), full(b0.shape),
            full(wkv.shape), full(bkv.shape), full(wvbt.shape), full(bvbt.shape),
            full(wfb.shape), full(bfb.shape),
            pl.BlockSpec((D, QT_COLS), qcol),
            pl.BlockSpec((1, QT_COLS), qcol),
        ],
        out_specs=[
            pl.BlockSpec((1, P, KVA_W), lambda b, j: (b, 0, 0)),
            pl.BlockSpec((1, P2, W_B), lambda b, j: (b, 0, 0)),
            pl.BlockSpec((1, W_B, P2), lambda b, j: (b, 0, 0)),
            pl.BlockSpec((1, P2, LANES), lambda b, j: (b, 0, 0)),
            pl.BlockSpec((1, S, QT_COLS), lambda b, j: (b, 0, jnp.maximum(j - 1, 0))),
        ],
        out_shape=[
            jax.ShapeDtypeStruct((B, P, KVA_W), BF16),
            jax.ShapeDtypeStruct((B, P2, W_B), BF16),
            jax.ShapeDtypeStruct((B, W_B, P2), BF16),
            jax.ShapeDtypeStruct((B, P2, LANES), F32),
            jax.ShapeDtypeStruct((B, S, wq.shape[1]), BF16),
        ],
        scratch_shapes=[pltpu.VMEM((P, D), BF16)],
        compiler_params=pltpu.CompilerParams(
            dimension_semantics=("arbitrary", "arbitrary"), vmem_limit_bytes=VMEM_LIMIT_INPROJ),
        name="inproj",
    )(x, meta, g0, b0, wkv, bkv, wvbt, bvbt, wfb, bfb, wq, bq)


def _t5_bucket_table():
    qi = np.arange(BLOCK)[:, None] + BLOCK
    kj = np.arange(2 * BLOCK)[None, :]
    n = np.maximum(qi - kj, 0)
    max_exact = T5_BUCKETS // 2
    nf = np.maximum(n, 1).astype(np.float32)
    large = max_exact + (np.log(nf / np.float32(max_exact)) / np.float32(math.log(T5_MAX_DIST / max_exact))
                         * (T5_BUCKETS - max_exact)).astype(np.int32)
    large = np.minimum(large, T5_BUCKETS - 1)
    bucket = np.where(n < max_exact, n, large).astype(np.int32)
    dist = qi - kj
    window = (dist >= 0) & (dist < WINDOW)
    return np.where(window, bucket, -1).astype(np.int32)


def _swa_kernel(bucket_ref, rel_ref, sink_ref, q_ref, kvp_ref, kvc_ref, o_ref, bias_ref):
    b = pl.program_id(0)
    n = pl.program_id(1)

    @pl.when((b == 0) & (n == 0))
    def _():
        bucket = bucket_ref[...]
        for h in range(N_HEADS_A):
            acc = jnp.where(bucket < 0, NEG, 0.0).astype(F32)
            for t in range(T5_BUCKETS):
                acc = jnp.where(bucket == t, rel_ref[t, h] * LOG2E, acc)
            bias_ref[h] = acc

    kcol = lax.broadcasted_iota(jnp.int32, (BLOCK, 2 * BLOCK), 1)
    kabs_ok = (n * BLOCK + kcol) >= PAD
    pw = 2 * HEAD_DIM
    first = lax.broadcasted_iota(jnp.int32, (BLOCK, pw), 1) < HEAD_DIM
    vfirst = lax.broadcasted_iota(jnp.int32, (2 * BLOCK, pw), 1) < HEAD_DIM
    for g in range(N_KV_A):
        kk = jnp.concatenate([kvp_ref[0, :, g * pw:(g + 1) * pw],
                              kvc_ref[0, :, g * pw:(g + 1) * pw]], axis=0)
        vv = jnp.concatenate([kvp_ref[0, :, (N_KV_A + g) * pw:(N_KV_A + g + 1) * pw],
                              kvc_ref[0, :, (N_KV_A + g) * pw:(N_KV_A + g + 1) * pw]], axis=0)
        vzero = jnp.zeros_like(vv)
        vm = (jnp.where(vfirst, vv, vzero), jnp.where(vfirst, vzero, vv))
        for pp in range(GROUP_A // 2):
            pair = g * (GROUP_A // 2) + pp
            q = q_ref[0, :, pair * pw:(pair + 1) * pw]
            qzero = jnp.zeros_like(q)
            qm = (jnp.where(first, q, qzero), jnp.where(first, qzero, q))
            out = None
            for i in range(2):
                h = 2 * pair + i
                s = _dot_nt(qm[i], kk) + bias_ref[h]
                s = jnp.where(kabs_ok, s, NEG)
                sink = sink_ref[h] * LOG2E
                m = jnp.maximum(jnp.max(s, axis=-1, keepdims=True), sink)
                p = jnp.exp2(s - m)
                denom = jnp.sum(p, axis=-1, keepdims=True) + jnp.exp2(sink - m)
                o = _dot(p.astype(BF16), vm[i]) / denom
                out = o if out is None else out + o
            o_ref[0, :, pair * pw:(pair + 1) * pw] = out.astype(BF16)


def _swa(zq, kva, rel_bias, sinks):
    B, S, _ = zq.shape
    nb = S // BLOCK
    bucket = jnp.asarray(_t5_bucket_table())
    smem = pl.BlockSpec(memory_space=pltpu.SMEM)
    return pl.pallas_call(
        _swa_kernel,
        grid=(B, nb),
        in_specs=[
            pl.BlockSpec((BLOCK, 2 * BLOCK), lambda b, n: (0, 0)),
            smem, smem,
            pl.BlockSpec((1, BLOCK, W_QA), lambda b, n: (b, n, 0)),
            pl.BlockSpec((1, BLOCK, KVA_W), lambda b, n: (b, n, 0)),
            pl.BlockSpec((1, BLOCK, KVA_W), lambda b, n: (b, n + 1, 0)),
        ],
        out_specs=pl.BlockSpec((1, BLOCK, W_QA), lambda b, n: (b, n, 0)),
        out_shape=jax.ShapeDtypeStruct((B, S, W_QA), BF16),
        scratch_shapes=[pltpu.VMEM((N_HEADS_A, BLOCK, 2 * BLOCK), F32)],
        compiler_params=pltpu.CompilerParams(
            dimension_semantics=("arbitrary", "arbitrary"), vmem_limit_bytes=VMEM_LIMIT),
        name="swa",
    )(bucket, rel_bias, sinks, zq, kva, kva)


FOX_TQ = 256
FOX_TK = 512
FOX_PAIRS = 4
C_MASKED = 1e30


def _fox_kernel(q_ref, k_ref, vt_ref, c_ref, o_ref):
    grp = pl.program_id(1)
    qi = pl.program_id(2)
    tq = q_ref.shape[1]
    tk = FOX_TK
    pw = 2 * HEAD_DIM
    q0 = BLOCK + qi * tq
    first = lax.broadcasted_iota(jnp.int32, (tq, pw), 1) < HEAD_DIM
    qm = []
    for pr in range(FOX_PAIRS):
        q = q_ref[0, :, pr * pw:(pr + 1) * pw]
        qzero = jnp.zeros_like(q)
        qm.append((jnp.where(first, q, qzero), jnp.where(first, qzero, q)))
    kpos = lax.broadcasted_iota(jnp.int32, (tk, tq), 0)
    qpos = q0 + lax.broadcasted_iota(jnp.int32, (tk, tq), 1)
    top = lax.broadcasted_iota(jnp.int32, (pw, tq), 0) < HEAD_DIM

    def ccols(start, size):
        c = c_ref[0, pl.ds(start, size), :]
        head = lax.broadcasted_iota(jnp.int32, c.shape, 1)
        return [[jnp.sum(jnp.where(head == 2 * (grp * FOX_PAIRS + pr) + i, c, 0.0), axis=1, keepdims=True)
                 for i in range(2)] for pr in range(FOX_PAIRS)]

    cq0 = [[col[0:1, :] for col in cols] for cols in ccols(pl.multiple_of(q0, BLOCK), 8)]

    def chunk(jc, carry, causal):
        ks = pl.multiple_of(jc * tk, tk)
        cc = ccols(ks, tk)
        out = []
        for pr in range(FOX_PAIRS):
            m, l, acc = carry[pr]
            kk = k_ref[0, pl.ds(ks, tk), pr * pw:(pr + 1) * pw]
            vt = vt_ref[0, pr * pw:(pr + 1) * pw, pl.ds(ks, tk)]
            vtop = lax.broadcasted_iota(jnp.int32, vt.shape, 0) < HEAD_DIM
            vzero = jnp.zeros_like(vt)
            vtm = (jnp.where(vtop, vt, vzero), jnp.where(vtop, vzero, vt))
            m_out, l_out, alphas, pv = [], [], [], None
            for i in range(2):
                s = _dot_nt(kk, qm[pr][i]) + (cq0[pr][i] - cc[pr][i]) * LOG2E
                if causal:
                    s = jnp.where(ks + kpos <= qpos, s, NEG)
                m_new = jnp.maximum(m[i], jnp.max(s, axis=0, keepdims=True))
                alpha = jnp.exp2(m[i] - m_new)
                p = jnp.exp2(s - m_new)
                m_out.append(m_new)
                l_out.append(alpha * l[i] + jnp.sum(p, axis=0, keepdims=True))
                alphas.append(alpha)
                d = _dot(vtm[i], p.astype(BF16))
                pv = d if pv is None else pv + d
            acc = jnp.where(top, alphas[0], alphas[1]) * acc + pv
            out.append((tuple(m_out), tuple(l_out), acc))
        return tuple(out)

    n_full = q0 // tk
    n_chunks = (q0 + tq + tk - 1) // tk
    neg = jnp.full((1, tq), NEG, F32)
    zero = jnp.zeros((1, tq), F32)
    carry = tuple(((neg, neg), (zero, zero), jnp.zeros((pw, tq), F32)) for _ in range(FOX_PAIRS))
    carry = lax.fori_loop(0, n_full, functools.partial(chunk, causal=False), carry)
    carry = lax.fori_loop(n_full, n_chunks, functools.partial(chunk, causal=True), carry)
    for pr in range(FOX_PAIRS):
        m, l, acc = carry[pr]
        o_ref[0, :, pr * pw:(pr + 1) * pw] = (acc / jnp.where(top, l[0], l[1])).T.astype(BF16)


def _fox(zq, kb, vbt, c):
    B, S, _ = zq.shape
    P = kb.shape[1]
    gw = FOX_PAIRS * 2 * HEAD_DIM
    ngrp = W_B // gw
    qblk0 = W_QA // gw
    return pl.pallas_call(
        _fox_kernel,
        grid=(B, ngrp, S // FOX_TQ),
        in_specs=[
            pl.BlockSpec((1, FOX_TQ, gw), lambda b, p, i: (b, i, qblk0 + p)),
            pl.BlockSpec((1, P, gw), lambda b, p, i: (b, 0, p)),
            pl.BlockSpec((1, gw, P), lambda b, p, i: (b, p, 0)),
            pl.BlockSpec((1, P, LANES), lambda b, p, i: (b, 0, 0)),
        ],
        out_specs=pl.BlockSpec((1, FOX_TQ, gw), lambda b, p, i: (b, i, p)),
        out_shape=jax.ShapeDtypeStruct((B, S, W_B), BF16),
        compiler_params=pltpu.CompilerParams(
            dimension_semantics=("arbitrary", "arbitrary", "arbitrary"), vmem_limit_bytes=VMEM_LIMIT),
        name="fox",
    )(zq, kb, vbt, c)


MERGE_TM = 512


def _merge_kernel(x_ref, oa_ref, ob_ref, ga_ref, gb_ref, g0_ref, b0_ref, wpa_ref, wpb_ref, wo_ref,
                  g1_ref, b1_ref, h1_ref, h1t_ref):
    h = _ln(x_ref[...], g0_ref[...], b0_ref[...])
    ya = _dot(oa_ref[...], wpa_ref[...])
    yb = _dot(ob_ref[...], wpb_ref[...])
    merged = jax.nn.sigmoid(ga_ref[...].astype(F32)) * ya + jax.nn.sigmoid(gb_ref[...].astype(F32)) * yb
    y = _dot(merged.astype(BF16), wo_ref[...])
    h1 = _ln(DEEPNORM_ALPHA * h + y, g1_ref[...], b1_ref[...])
    h1_ref[...] = h1
    h1t_ref[...] = h1.T.astype(BF16)


def _merge(x2, oa2, ob2, zq2, g0, b0, wpa, wpb, wo, g1, b1):
    T, D = x2.shape
    tm = MERGE_TM
    full = lambda a: pl.BlockSpec(a.shape, lambda i: (0,) * a.ndim)
    gate0 = (W_QA + W_B) // D
    return pl.pallas_call(
        _merge_kernel,
        grid=(T // tm,),
        in_specs=[
            pl.BlockSpec((tm, D), lambda i: (i, 0)),
            pl.BlockSpec((tm, W_QA), lambda i: (i, 0)),
            pl.BlockSpec((tm, W_B), lambda i: (i, 0)),
            pl.BlockSpec((tm, D), lambda i: (i, gate0)),
            pl.BlockSpec((tm, D), lambda i: (i, gate0 + 1)),
            full(g0), full(b0), full(wpa), full(wpb), full(wo), full(g1), full(b1),
        ],
        out_specs=[pl.BlockSpec((tm, D), lambda i: (i, 0)), pl.BlockSpec((D, tm), lambda i: (0, i))],
        out_shape=[jax.ShapeDtypeStruct((T, D), F32), jax.ShapeDtypeStruct((D, T), BF16)],
        compiler_params=pltpu.CompilerParams(
            dimension_semantics=("arbitrary",), vmem_limit_bytes=VMEM_LIMIT),
        name="merge",
    )(x2, oa2, ob2, zq2, zq2, g0, b0, wpa, wpb, wo, g1, b1)


ROUTE_TM = 512
ROUTE_UNROLL = 2
LANES = 128
HALF_DQ = PEER_DQ // 2
RANK_OUT = 64.0


def _top_sorted(s, with_rank):
    rank = jnp.full(s.shape, RANK_OUT, F32) if with_rank else None
    tops = []
    for r in range(PEER_TOPK):
        m = jnp.max(s, axis=0, keepdims=True)
        eq = s == m
        if with_rank:
            rank = jnp.where(eq, float(r), rank)
        s = jnp.where(eq, -jnp.inf, s)
        tops.append(m)
    return tops, rank


def _stack_rows(rows):
    n = len(rows)
    ri = lax.broadcasted_iota(jnp.int32, (n, rows[0].shape[1]), 0)
    out = jnp.broadcast_to(rows[0], ri.shape)
    for r in range(1, n):
        out = jnp.where(ri == r, rows[r], out)
    return out


def _route_kernel(ht_ref, wqt_ref, sk_ref, na_ref, e1_ref, rb_ref, e2_ref, q_ref, sc_ref):
    K = PEER_TOPK
    q_ref[...] = _dot(wqt_ref[...], ht_ref[...]).astype(BF16)
    for h in range(PEER_HEADS):
        sc_ref[h] = _dot(sk_ref[h], q_ref[h * PEER_DQ:(h + 1) * PEER_DQ, :])

    n_groups = ht_ref.shape[1] // LANES

    def one_group(h, g):
        lanes = pl.ds(pl.multiple_of(g * LANES, LANES), LANES)
        s1 = sc_ref[h, :PEER_NKEYS, lanes]
        s2 = sc_ref[h, PEER_NKEYS:, lanes]
        a, _ = _top_sorted(s1, with_rank=False)
        b, rb = _top_sorted(s2, with_rank=True)
        bmat16 = _stack_rows(b)
        bmat8 = bmat16[:8]
        cands = [a[0] + bmat16]
        for i in range(1, 8):
            cands.append(a[i] + bmat8)
        cands.append(_stack_rows(a[8:]) + b[0])
        cand = jnp.concatenate(cands, axis=0)
        vmax = a[0] + b[0]
        z = jnp.zeros_like(vmax)
        for r in range(K):
            m = jnp.max(cand, axis=0, keepdims=True)
            cand = jnp.where(cand == m, -jnp.inf, cand)
            z = z + jnp.exp(m - vmax)
        sel = jnp.where(cand == -jnp.inf, 1.0, 0.0)
        counts = [jnp.sum(sel[0:16], axis=0, keepdims=True)]
        for i in range(1, 8):
            counts.append(jnp.sum(sel[8 + 8 * i:16 + 8 * i], axis=0, keepdims=True))
        n_tail = jnp.sum(sel[72:80], axis=0, keepdims=True)
        low = jnp.full_like(n_tail, jnp.inf)
        for k in range(1, 9):
            low = jnp.where(n_tail == float(k), a[7 + k], low)
        na = jnp.where(s1 >= low, jnp.where(s1 < a[7], 1.0, 0.0), 0.0)
        for i in range(8):
            na = jnp.where(s1 == a[i], counts[i], na)
        na_ref[h, :, lanes] = na
        e1_ref[h, :, lanes] = jnp.exp(s1 - a[0])
        rb_ref[h, :, lanes] = rb.astype(BF16)
        e2_ref[h, :, lanes] = (jnp.exp(s2 - b[0]) * (0.5 / z)).astype(BF16)

    per_head = n_groups // ROUTE_UNROLL

    def group(idx, _):
        h = idx // per_head
        g0 = (idx % per_head) * ROUTE_UNROLL
        for k in range(ROUTE_UNROLL):
            one_group(h, g0 + k)
        return 0

    lax.fori_loop(0, PEER_HEADS * per_head, group, 0)


def _route(h1t, wqt, skbd):
    D, T = h1t.shape
    tm = ROUTE_TM
    full = lambda a: pl.BlockSpec(a.shape, lambda i: (0,) * a.ndim)
    ospec = pl.BlockSpec((PEER_HEADS, PEER_NKEYS, tm), lambda i: (0, 0, i))
    oshape = lambda dt: jax.ShapeDtypeStruct((PEER_HEADS, PEER_NKEYS, T), dt)
    return pl.pallas_call(
        _route_kernel,
        grid=(T // tm,),
        in_specs=[pl.BlockSpec((D, tm), lambda i: (0, i)), full(wqt), full(skbd)],
        out_specs=[ospec, ospec, ospec, ospec],
        out_shape=[oshape(F32), oshape(F32), oshape(BF16), oshape(BF16)],
        scratch_shapes=[pltpu.VMEM((PEER_HEADS * PEER_DQ, tm), BF16),
                        pltpu.VMEM((PEER_HEADS, 2 * PEER_NKEYS, tm), F32)],
        compiler_params=pltpu.CompilerParams(
            dimension_semantics=("arbitrary",), vmem_limit_bytes=VMEM_LIMIT),
        name="route",
    )(h1t, wqt, skbd)


EXP_TM = 512
EXP_TE = 2048
EXP_IA = EXP_TE // PEER_NKEYS
BF16_ROWS = 16
SQRT_HALF = 0.7071067811865476


def _expert_kernel(xt_ref, h1_ref, u_ref, v_ref, na_ref, e1_ref, rb_ref, e2_ref, g2_ref, b2_ref,
                   o_ref, acc_ref, hu_ref, a_ref):
    j = pl.program_id(1)
    nj = pl.num_programs(1)
    tm = xt_ref.shape[1]
    nsub = PEER_NKEYS // BF16_ROWS

    @pl.when(j == 0)
    def _():
        acc_ref[...] = jnp.zeros_like(acc_ref)

    hu_ref[...] = _dot(u_ref[...], xt_ref[...])
    rows = pl.ds(pl.multiple_of(j * EXP_IA, EXP_IA), EXP_IA)
    for il in range(EXP_IA):
        sl = slice(il * PEER_NKEYS, (il + 1) * PEER_NKEYS)
        hu = hu_ref[sl, :]
        hb = hu.astype(BF16)
        act = hb * (1.0 + lax.erf(hb * SQRT_HALF))
        gate = jnp.zeros((nsub, BF16_ROWS, tm), BF16)
        for h in range(PEER_HEADS):
            na = jnp.broadcast_to(na_ref[h, rows, :][il:il + 1], (BF16_ROWS, tm)).astype(BF16)
            e1 = jnp.broadcast_to(e1_ref[h, rows, :][il:il + 1], (BF16_ROWS, tm)).astype(BF16)
            rb = rb_ref[h].reshape(nsub, BF16_ROWS, tm)
            w = e2_ref[h].reshape(nsub, BF16_ROWS, tm) * e1[None]
            gate = gate + jnp.where(rb < na[None], w, jnp.zeros_like(w))
        a_ref[sl, :] = act * gate.reshape(PEER_NKEYS, tm)
    acc_ref[...] += _dot_tn(a_ref[...], v_ref[...])

    @pl.when(j == nj - 1)
    def _():
        o_ref[...] = _ln(DEEPNORM_ALPHA * h1_ref[...] + acc_ref[...], g2_ref[...], b2_ref[...])


def _experts(h1t, h1, u, v, na, e1, rb, e2, g2, b2):
    T, D = h1.shape
    E = u.shape[0]
    tm, te = EXP_TM, EXP_TE
    gspec = pl.BlockSpec((PEER_HEADS, PEER_NKEYS, tm), lambda i, j: (0, 0, i))
    vec = pl.BlockSpec((1, D), lambda i, j: (0, 0))
    return pl.pallas_call(
        _expert_kernel,
        grid=(T // tm, E // te),
        in_specs=[
            pl.BlockSpec((D, tm), lambda i, j: (0, i)),
            pl.BlockSpec((tm, D), lambda i, j: (i, 0)),
            pl.BlockSpec((te, D), lambda i, j: (j, 0)),
            pl.BlockSpec((te, D), lambda i, j: (j, 0)),
            gspec, gspec, gspec, gspec, vec, vec,
        ],
        out_specs=pl.BlockSpec((tm, D), lambda i, j: (i, 0)),
        out_shape=jax.ShapeDtypeStruct((T, D), F32),
        scratch_shapes=[pltpu.VMEM((tm, D), F32), pltpu.VMEM((te, tm), F32), pltpu.VMEM((te, tm), BF16)],
        compiler_params=pltpu.CompilerParams(
            dimension_semantics=("arbitrary", "arbitrary"), vmem_limit_bytes=VMEM_LIMIT),
        name="experts",
    )(h1t, h1, u, v, na, e1, rb, e2, g2, b2)


def kernel(x, meta_tokens, emb_ln_g, emb_ln_b, rel_bias, w_in, b_in, attn_sinks, w_pa, w_pb, w_o,
           ln1_g, ln1_b, peer_wq, peer_subkeys, peer_u, peer_v, ln2_g, ln2_b):
    B, S, D = x.shape
    T = B * S
    row = lambda a: a.reshape(1, -1).astype(F32)

    w, bi = w_in[0], b_in[0]
    sizes = (W_QA, W_KA, W_KA, W_B, W_B, W_B, N_HEADS_B, D, D)
    offs = np.cumsum((0,) + sizes)
    col = lambda a, k: a[..., offs[k]:offs[k + 1]]
    scale = HEAD_DIM ** -0.5 * LOG2E
    wq = jnp.concatenate([col(w, 0) * scale, col(w, 3) * scale, col(w, 7), col(w, 8)], axis=-1).astype(BF16)
    bq = jnp.concatenate([col(bi, 0) * scale, col(bi, 3) * scale, col(bi, 7), col(bi, 8)], axis=-1)
    dup = lambda a: jnp.repeat(a.reshape(a.shape[:-1] + (N_KV_A, 1, HEAD_DIM)), 2, axis=-2).reshape(
        a.shape[:-1] + (2 * W_KA,))
    wkv = jnp.concatenate([dup(col(w, 1)), dup(col(w, 2)), col(w, 4)], axis=-1).astype(BF16)
    bkv = jnp.concatenate([dup(col(bi, 1)), dup(col(bi, 2)), col(bi, 4)], axis=-1)
    wvbt = col(w, 5).T.astype(BF16)
    bvbt = col(bi, 5).reshape(-1, 1).astype(F32)
    fpad = LANES - N_HEADS_B
    wfb = jnp.pad(col(w, 6), ((0, 0), (0, fpad))).astype(BF16)
    bfb = jnp.pad(col(bi, 6), (0, fpad))

    g0, b0 = row(emb_ln_g), row(emb_ln_b)
    kva, kb, vbt, c, zq = _inproj(x, meta_tokens.astype(F32), g0, b0, wkv, row(bkv), wvbt, bvbt,
                                  wfb, row(bfb), wq, row(bq))
    oa = _swa(zq, kva, rel_bias.astype(F32), attn_sinks[0].astype(F32))
    ob = _fox(zq, kb, vbt, c)

    h1, h1t = _merge(x.reshape(T, D), oa.reshape(T, W_QA), ob.reshape(T, W_B), zq.reshape(T, -1),
                     g0, b0, w_pa[0].astype(BF16), w_pb[0].astype(BF16), w_o[0].astype(BF16),
                     row(ln1_g[0]), row(ln1_b[0]))

    wqt = peer_wq[0].T.astype(BF16)
    sk = peer_subkeys[0].astype(BF16)
    zero = jnp.zeros_like(sk[:, 0])
    skbd = jnp.concatenate([jnp.concatenate([sk[:, 0], zero], axis=-1),
                            jnp.concatenate([zero, sk[:, 1]], axis=-1)], axis=1)
    na, e1, rb, e2 = _route(h1t, wqt, skbd)
    out = _experts(h1t, h1, peer_u[0].astype(BF16), peer_v[0].astype(BF16), na, e1, rb, e2,
                   row(ln2_g[0]), row(ln2_b[0]))
    return out.reshape(B, S, D)
```

```python
import functools
import math

import jax
import jax.numpy as jnp
import numpy as np
from jax import lax
from jax.experimental import pallas as pl
from jax.experimental.pallas import tpu as pltpu

N_META = 16
BLOCK = 128
PAD = BLOCK - N_META
HEAD_DIM = 64
N_HEADS_A = 8
N_KV_A = 2
GROUP_A = N_HEADS_A // N_KV_A
WINDOW = 128
N_HEADS_B = 8
T5_BUCKETS = 32
T5_MAX_DIST = 128
PEER_HEADS = 8
PEER_NKEYS = 128
PEER_DQ = 256
PEER_TOPK = 16
LN_EPS = 1e-5
NEG = -1e30
LOG2E = 1.4426950408889634
DEPTH = 1
DEEPNORM_ALPHA = (2 * DEPTH) ** 0.25

W_QA = N_HEADS_A * HEAD_DIM
W_KA = N_KV_A * HEAD_DIM
W_B = N_HEADS_B * HEAD_DIM
KVA_W = 4 * W_KA

F32 = jnp.float32
BF16 = jnp.bfloat16

VMEM_LIMIT = 56 * 1024 * 1024
VMEM_LIMIT_INPROJ = 60 * 1024 * 1024


def _ln(x, g, b):
    mu = jnp.mean(x, axis=-1, keepdims=True)
    xc = x - mu
    var = jnp.mean(xc * xc, axis=-1, keepdims=True)
    return xc * lax.rsqrt(var + LN_EPS) * g + b


def _dot(a, b):
    return jnp.dot(a, b, preferred_element_type=F32)


def _dot_nt(a, b):
    return lax.dot_general(a, b, (((1,), (1,)), ((), ())), preferred_element_type=F32)


def _dot_tn(a, b):
    return lax.dot_general(a, b, (((0,), (0,)), ((), ())), preferred_element_type=F32)


def _split3(x):
    hi = x.astype(BF16)
    r = x - hi.astype(F32)
    mid = r.astype(BF16)
    lo = (r - mid.astype(F32)).astype(BF16)
    return hi, mid, lo


QT_COLS = 512


def _inproj_kernel(x_ref, meta_ref, g0_ref, b0_ref, wkv_ref, bkv_ref, wvbt_ref, bvbt_ref,
                   wfb_ref, bfb_ref, wq_ref, bq_ref,
                   kva_ref, kb_ref, vbt_ref, c_ref, zq_ref, hs_ref):
    j = pl.program_id(1)
    P = hs_ref.shape[0]

    @pl.when(j == 0)
    def _():
        g0 = g0_ref[...]
        b0 = b0_ref[...]
        hs_ref[0:PAD, :] = jnp.zeros((PAD, hs_ref.shape[1]), BF16)
        hs_ref[PAD:BLOCK, :] = _ln(meta_ref[...], g0, b0).astype(BF16)
        hs_ref[BLOCK:, :] = _ln(x_ref[0], g0, b0).astype(BF16)
        hs = hs_ref[...]
        nkv = KVA_W
        kva_ref[0] = (_dot(hs, wkv_ref[:, :nkv]) + bkv_ref[:, :nkv]).astype(BF16)
        kb_ref[0, :P, :] = (_dot(hs, wkv_ref[:, nkv:]) + bkv_ref[:, nkv:]).astype(BF16)
        kb_ref[0, P:, :] = jnp.zeros((kb_ref.shape[1] - P, kb_ref.shape[2]), BF16)
        vbt_ref[0, :, :P] = (_dot_nt(wvbt_ref[...], hs) + bvbt_ref[...]).astype(BF16)
        vbt_ref[0, :, P:] = jnp.zeros((vbt_ref.shape[1], vbt_ref.shape[2] - P), BF16)
        c_ref[0, P:, :] = jnp.full((c_ref.shape[1] - P, c_ref.shape[2]), C_MASKED, F32)
        lf = jax.nn.log_sigmoid(_dot(hs, wfb_ref[...]) + bfb_ref[...])
        pos = lax.broadcasted_iota(jnp.int32, lf.shape, 0)
        lf = jnp.where(pos >= PAD, lf, 0.0)
        ri = lax.broadcasted_iota(jnp.int32, (BLOCK, BLOCK), 0)
        ci = lax.broadcasted_iota(jnp.int32, (BLOCK, BLOCK), 1)
        tri = jnp.where(ci <= ri, 1.0, 0.0).astype(BF16)
        carry = jnp.zeros((1, lf.shape[1]), F32)
        for n in range(P // BLOCK):
            hi, mid, lo = _split3(lf[n * BLOCK:(n + 1) * BLOCK, :])
            cb = _dot(tri, hi) + _dot(tri, mid) + _dot(tri, lo) + carry
            carry = cb[BLOCK - 1:BLOCK, :]
            if n == 0:
                cb = jnp.where(lax.broadcasted_iota(jnp.int32, cb.shape, 0) >= PAD, cb, C_MASKED)
            c_ref[0, n * BLOCK:(n + 1) * BLOCK, :] = cb

    @pl.when(j > 0)
    def _():
        zq_ref[0] = (_dot(hs_ref[BLOCK:, :], wq_ref[...]) + bq_ref[...]).astype(BF16)


def _inproj(x, meta, g0, b0, wkv, bkv, wvbt, bvbt, wfb, bfb, wq, bq):
    B, S, D = x.shape
    P = S + BLOCK
    P2 = -(-P // FOX_TK) * FOX_TK
    nq = wq.shape[1] // QT_COLS
    full = lambda shape: pl.BlockSpec(shape, lambda b, j: (0,) * len(shape))
    qcol = lambda b, j: (0, jnp.maximum(j - 1, 0))
    return pl.pallas_call(
        _inproj_kernel,
        grid=(B, nq + 1),
        in_specs=[
            pl.BlockSpec((1, S, D), lambda b, j: (b, 0, 0)),
            full(meta.shape), full(g0.shape), full(b0.shape),
            full(wkv.shape), full(bkv.shape), full(wvbt.shape), full(bvbt.shape),
            full(wfb.shape), full(bfb.shape),
            pl.BlockSpec((D, QT_COLS), qcol),
            pl.BlockSpec((1, QT_COLS), qcol),
        ],
        out_specs=[
            pl.BlockSpec((1, P, KVA_W), lambda b, j: (b, 0, 0)),
            pl.BlockSpec((1, P2, W_B), lambda b, j: (b, 0, 0)),
            pl.BlockSpec((1, W_B, P2), lambda b, j: (b, 0, 0)),
            pl.BlockSpec((1, P2, LANES), lambda b, j: (b, 0, 0)),
            pl.BlockSpec((1, S, QT_COLS), lambda b, j: (b, 0, jnp.maximum(j - 1, 0))),
        ],
        out_shape=[
            jax.ShapeDtypeStruct((B, P, KVA_W), BF16),
            jax.ShapeDtypeStruct((B, P2, W_B), BF16),
            jax.ShapeDtypeStruct((B, W_B, P2), BF16),
            jax.ShapeDtypeStruct((B, P2, LANES), F32),
            jax.ShapeDtypeStruct((B, S, wq.shape[1]), BF16),
        ],
        scratch_shapes=[pltpu.VMEM((P, D), BF16)],
        compiler_params=pltpu.CompilerParams(
            dimension_semantics=("arbitrary", "arbitrary"), vmem_limit_bytes=VMEM_LIMIT_INPROJ),
        name="inproj",
    )(x, meta, g0, b0, wkv, bkv, wvbt, bvbt, wfb, bfb, wq, bq)


def _t5_bucket_table():
    qi = np.arange(BLOCK)[:, None] + BLOCK
    kj = np.arange(2 * BLOCK)[None, :]
    n = np.maximum(qi - kj, 0)
    max_exact = T5_BUCKETS // 2
    nf = np.maximum(n, 1).astype(np.float32)
    large = max_exact + (np.log(nf / np.float32(max_exact)) / np.float32(math.log(T5_MAX_DIST / max_exact))
                         * (T5_BUCKETS - max_exact)).astype(np.int32)
    large = np.minimum(large, T5_BUCKETS - 1)
    bucket = np.where(n < max_exact, n, large).astype(np.int32)
    dist = qi - kj
    window = (dist >= 0) & (dist < WINDOW)
    return np.where(window, bucket, -1).astype(np.int32)


def _swa_kernel(bucket_ref, rel_ref, sink_ref, q_ref, kvp_ref, kvc_ref, o_ref, bias_ref):
    b = pl.program_id(0)
    n = pl.program_id(1)

    @pl.when((b == 0) & (n == 0))
    def _():
        bucket = bucket_ref[...]
        for h in range(N_HEADS_A):
            acc = jnp.where(bucket < 0, NEG, 0.0).astype(F32)
            for t in range(T5_BUCKETS):
                acc = jnp.where(bucket == t, rel_ref[t, h] * LOG2E, acc)
            bias_ref[h] = acc

    kcol = lax.broadcasted_iota(jnp.int32, (BLOCK, 2 * BLOCK), 1)
    kabs_ok = (n * BLOCK + kcol) >= PAD
    pw = 2 * HEAD_DIM
    first = lax.broadcasted_iota(jnp.int32, (BLOCK, pw), 1) < HEAD_DIM
    vfirst = lax.broadcasted_iota(jnp.int32, (2 * BLOCK, pw), 1) < HEAD_DIM
    for g in range(N_KV_A):
        kk = jnp.concatenate([kvp_ref[0, :, g * pw:(g + 1) * pw],
                              kvc_ref[0, :, g * pw:(g + 1) * pw]], axis=0)
        vv = jnp.concatenate([kvp_ref[0, :, (N_KV_A + g) * pw:(N_KV_A + g + 1) * pw],
                              kvc_ref[0, :, (N_KV_A + g) * pw:(N_KV_A + g + 1) * pw]], axis=0)
        vzero = jnp.zeros_like(vv)
        vm = (jnp.where(vfirst, vv, vzero), jnp.where(vfirst, vzero, vv))
        for pp in range(GROUP_A // 2):
            pair = g * (GROUP_A // 2) + pp
            q = q_ref[0, :, pair * pw:(pair + 1) * pw]
            qzero = jnp.zeros_like(q)
            qm = (jnp.where(first, q, qzero), jnp.where(first, qzero, q))
            out = None
            for i in range(2):
                h = 2 * pair + i
                s = _dot_nt(qm[i], kk) + bias_ref[h]
                s = jnp.where(kabs_ok, s, NEG)
                sink = sink_ref[h] * LOG2E
                m = jnp.maximum(jnp.max(s, axis=-1, keepdims=True), sink)
                p = jnp.exp2(s - m)
                denom = jnp.sum(p, axis=-1, keepdims=True) + jnp.exp2(sink - m)
                o = _dot(p.astype(BF16), vm[i]) / denom
                out = o if out is None else out + o
            o_ref[0, :, pair * pw:(pair + 1) * pw] = out.astype(BF16)


def _swa(zq, kva, rel_bias, sinks):
    B, S, _ = zq.shape
    nb = S // BLOCK
    bucket = jnp.asarray(_t5_bucket_table())
    smem = pl.BlockSpec(memory_space=pltpu.SMEM)
    return pl.pallas_call(
        _swa_kernel,
        grid=(B, nb),
        in_specs=[
            pl.BlockSpec((BLOCK, 2 * BLOCK), lambda b, n: (0, 0)),
            smem, smem,
            pl.BlockSpec((1, BLOCK, W_QA), lambda b, n: (b, n, 0)),
            pl.BlockSpec((1, BLOCK, KVA_W), lambda b, n: (b, n, 0)),
            pl.BlockSpec((1, BLOCK, KVA_W), lambda b, n: (b, n + 1, 0)),
        ],
        out_specs=pl.BlockSpec((1, BLOCK, W_QA), lambda b, n: (b, n, 0)),
        out_shape=jax.ShapeDtypeStruct((B, S, W_QA), BF16),
        scratch_shapes=[pltpu.VMEM((N_HEADS_A, BLOCK, 2 * BLOCK), F32)],
        compiler_params=pltpu.CompilerParams(
            dimension_semantics=("arbitrary", "arbitrary"), vmem_limit_bytes=VMEM_LIMIT),
        name="swa",
    )(bucket, rel_bias, sinks, zq, kva, kva)


FOX_TQ = 512
FOX_TK = 512
FOX_PAIRS = 4
C_MASKED = 1e30


def _fox_kernel(q_ref, k_ref, vt_ref, c_ref, o_ref):
    grp = pl.program_id(1)
    qi = pl.program_id(2)
    tq = q_ref.shape[1]
    tk = FOX_TK
    pw = 2 * HEAD_DIM
    q0 = BLOCK + qi * tq
    first = lax.broadcasted_iota(jnp.int32, (tq, pw), 1) < HEAD_DIM
    qm = []
    for pr in range(FOX_PAIRS):
        q = q_ref[0, :, pr * pw:(pr + 1) * pw]
        qzero = jnp.zeros_like(q)
        qm.append((jnp.where(first, q, qzero), jnp.where(first, qzero, q)))
    kpos = lax.broadcasted_iota(jnp.int32, (tk, tq), 0)
    qpos = q0 + lax.broadcasted_iota(jnp.int32, (tk, tq), 1)
    top = lax.broadcasted_iota(jnp.int32, (pw, tq), 0) < HEAD_DIM

    def ccols(start, size):
        c = c_ref[0, pl.ds(start, size), :]
        head = lax.broadcasted_iota(jnp.int32, c.shape, 1)
        return [[jnp.sum(jnp.where(head == 2 * (grp * FOX_PAIRS + pr) + i, c, 0.0), axis=1, keepdims=True)
                 for i in range(2)] for pr in range(FOX_PAIRS)]

    cq0 = [[col[0:1, :] for col in cols] for cols in ccols(pl.multiple_of(q0, BLOCK), 8)]

    def chunk(jc, carry, causal):
        ks = pl.multiple_of(jc * tk, tk)
        cc = ccols(ks, tk)
        out = []
        for pr in range(FOX_PAIRS):
            m, l, acc = carry[pr]
            kk = k_ref[0, pl.ds(ks, tk), pr * pw:(pr + 1) * pw]
            vt = vt_ref[0, pr * pw:(pr + 1) * pw, pl.ds(ks, tk)]
            vtop = lax.broadcasted_iota(jnp.int32, vt.shape, 0) < HEAD_DIM
            vzero = jnp.zeros_like(vt)
            vtm = (jnp.where(vtop, vt, vzero), jnp.where(vtop, vzero, vt))
            m_out, l_out, alphas, pv = [], [], [], None
            for i in range(2):
                s = _dot_nt(kk, qm[pr][i]) + (cq0[pr][i] - cc[pr][i]) * LOG2E
                if causal:
                    s = jnp.where(ks + kpos <= qpos, s, NEG)
                m_new = jnp.maximum(m[i], jnp.max(s, axis=0, keepdims=True))
                alpha = jnp.exp2(m[i] - m_new)
                p = jnp.exp2(s - m_new)
                m_out.append(m_new)
                l_out.append(alpha * l[i] + jnp.sum(p, axis=0, keepdims=True))
                alphas.append(alpha)
                d = _dot(vtm[i], p.astype(BF16))
                pv = d if pv is None else pv + d
            acc = jnp.where(top, alphas[0], alphas[1]) * acc + pv
            out.append((tuple(m_out), tuple(l_out), acc))
        return tuple(out)

    n_full = q0 // tk
    n_chunks = (q0 + tq + tk - 1) // tk
    neg = jnp.full((1, tq), NEG, F32)
    zero = jnp.zeros((1, tq), F32)
    carry = tuple(((neg, neg), (zero, zero), jnp.zeros((pw, tq), F32)) for _ in range(FOX_PAIRS))
    carry = lax.fori_loop(0, n_full, functools.partial(chunk, causal=False), carry)
    carry = lax.fori_loop(n_full, n_chunks, functools.partial(chunk, causal=True), carry)
    for pr in range(FOX_PAIRS):
        m, l, acc = carry[pr]
        o_ref[0, :, pr * pw:(pr + 1) * pw] = (acc / jnp.where(top, l[0], l[1])).T.astype(BF16)


def _fox(zq, kb, vbt, c):
    B, S, _ = zq.shape
    P = kb.shape[1]
    gw = FOX_PAIRS * 2 * HEAD_DIM
    ngrp = W_B // gw
    qblk0 = W_QA // gw
    return pl.pallas_call(
        _fox_kernel,
        grid=(B, ngrp, S // FOX_TQ),
        in_specs=[
            pl.BlockSpec((1, FOX_TQ, gw), lambda b, p, i: (b, i, qblk0 + p)),
            pl.BlockSpec((1, P, gw), lambda b, p, i: (b, 0, p)),
            pl.BlockSpec((1, gw, P), lambda b, p, i: (b, p, 0)),
            pl.BlockSpec((1, P, LANES), lambda b, p, i: (b, 0, 0)),
        ],
        out_specs=pl.BlockSpec((1, FOX_TQ, gw), lambda b, p, i: (b, i, p)),
        out_shape=jax.ShapeDtypeStruct((B, S, W_B), BF16),
        compiler_params=pltpu.CompilerParams(
            dimension_semantics=("arbitrary", "arbitrary", "arbitrary"), vmem_limit_bytes=VMEM_LIMIT),
        name="fox",
    )(zq, kb, vbt, c)


MERGE_TM = 512


def _merge_kernel(x_ref, oa_ref, ob_ref, ga_ref, gb_ref, g0_ref, b0_ref, wpa_ref, wpb_ref, wo_ref,
                  g1_ref, b1_ref, h1_ref, h1t_ref):
    h = _ln(x_ref[...], g0_ref[...], b0_ref[...])
    ya = _dot(oa_ref[...], wpa_ref[...])
    yb = _dot(ob_ref[...], wpb_ref[...])
    merged = jax.nn.sigmoid(ga_ref[...].astype(F32)) * ya + jax.nn.sigmoid(gb_ref[...].astype(F32)) * yb
    y = _dot(merged.astype(BF16), wo_ref[...])
    h1 = _ln(DEEPNORM_ALPHA * h + y, g1_ref[...], b1_ref[...])
    h1_ref[...] = h1
    h1t_ref[...] = h1.T.astype(BF16)


def _merge(x2, oa2, ob2, zq2, g0, b0, wpa, wpb, wo, g1, b1):
    T, D = x2.shape
    tm = MERGE_TM
    full = lambda a: pl.BlockSpec(a.shape, lambda i: (0,) * a.ndim)
    gate0 = (W_QA + W_B) // D
    return pl.pallas_call(
        _merge_kernel,
        grid=(T // tm,),
        in_specs=[
            pl.BlockSpec((tm, D), lambda i: (i, 0)),
            pl.BlockSpec((tm, W_QA), lambda i: (i, 0)),
            pl.BlockSpec((tm, W_B), lambda i: (i, 0)),
            pl.BlockSpec((tm, D), lambda i: (i, gate0)),
            pl.BlockSpec((tm, D), lambda i: (i, gate0 + 1)),
            full(g0), full(b0), full(wpa), full(wpb), full(wo), full(g1), full(b1),
        ],
        out_specs=[pl.BlockSpec((tm, D), lambda i: (i, 0)), pl.BlockSpec((D, tm), lambda i: (0, i))],
        out_shape=[jax.ShapeDtypeStruct((T, D), F32), jax.ShapeDtypeStruct((D, T), BF16)],
        compiler_params=pltpu.CompilerParams(
            dimension_semantics=("arbitrary",), vmem_limit_bytes=VMEM_LIMIT),
        name="merge",
    )(x2, oa2, ob2, zq2, zq2, g0, b0, wpa, wpb, wo, g1, b1)


ROUTE_TM = 512
ROUTE_UNROLL = 2
LANES = 128
HALF_DQ = PEER_DQ // 2
RANK_OUT = 64.0


def _top_sorted(s, with_rank):
    rank = jnp.full(s.shape, RANK_OUT, F32) if with_rank else None
    tops = []
    for r in range(PEER_TOPK):
        m = jnp.max(s, axis=0, keepdims=True)
        eq = s == m
        if with_rank:
            rank = jnp.where(eq, float(r), rank)
        s = jnp.where(eq, -jnp.inf, s)
        tops.append(m)
    return tops, rank


def _stack_rows(rows):
    n = len(rows)
    ri = lax.broadcasted_iota(jnp.int32, (n, rows[0].shape[1]), 0)
    out = jnp.broadcast_to(rows[0], ri.shape)
    for r in range(1, n):
        out = jnp.where(ri == r, rows[r], out)
    return out


def _route_kernel(ht_ref, wqt_ref, sk_ref, na_ref, e1_ref, rb_ref, e2_ref, q_ref, sc_ref):
    K = PEER_TOPK
    q_ref[...] = _dot(wqt_ref[...], ht_ref[...]).astype(BF16)
    for h in range(PEER_HEADS):
        sc_ref[h] = _dot(sk_ref[h], q_ref[h * PEER_DQ:(h + 1) * PEER_DQ, :])

    n_groups = ht_ref.shape[1] // LANES

    def one_group(h, g):
        lanes = pl.ds(pl.multiple_of(g * LANES, LANES), LANES)
        s1 = sc_ref[h, :PEER_NKEYS, lanes]
        s2 = sc_ref[h, PEER_NKEYS:, lanes]
        a, _ = _top_sorted(s1, with_rank=False)
        b, rb = _top_sorted(s2, with_rank=True)
        bmat16 = _stack_rows(b)
        bmat8 = bmat16[:8]
        cands = [a[0] + bmat16]
        for i in range(1, 8):
            cands.append(a[i] + bmat8)
        cands.append(_stack_rows(a[8:]) + b[0])
        cand = jnp.concatenate(cands, axis=0)
        vmax = a[0] + b[0]
        z = jnp.zeros_like(vmax)
        for r in range(K):
            m = jnp.max(cand, axis=0, keepdims=True)
            cand = jnp.where(cand == m, -jnp.inf, cand)
            z = z + jnp.exp(m - vmax)
        sel = jnp.where(cand == -jnp.inf, 1.0, 0.0)
        counts = [jnp.sum(sel[0:16], axis=0, keepdims=True)]
        for i in range(1, 8):
            counts.append(jnp.sum(sel[8 + 8 * i:16 + 8 * i], axis=0, keepdims=True))
        n_tail = jnp.sum(sel[72:80], axis=0, keepdims=True)
        low = jnp.full_like(n_tail, jnp.inf)
        for k in range(1, 9):
            low = jnp.where(n_tail == float(k), a[7 + k], low)
        na = jnp.where(s1 >= low, jnp.where(s1 < a[7], 1.0, 0.0), 0.0)
        for i in range(8):
            na = jnp.where(s1 == a[i], counts[i], na)
        na_ref[h, :, lanes] = na
        e1_ref[h, :, lanes] = jnp.exp(s1 - a[0])
        rb_ref[h, :, lanes] = rb.astype(BF16)
        e2_ref[h, :, lanes] = (jnp.exp(s2 - b[0]) * (0.5 / z)).astype(BF16)

    per_head = n_groups // ROUTE_UNROLL

    def group(idx, _):
        h = idx // per_head
        g0 = (idx % per_head) * ROUTE_UNROLL
        for k in range(ROUTE_UNROLL):
            one_group(h, g0 + k)
        return 0

    lax.fori_loop(0, PEER_HEADS * per_head, group, 0)


def _route(h1t, wqt, skbd):
    D, T = h1t.shape
    tm = ROUTE_TM
    full = lambda a: pl.BlockSpec(a.shape, lambda i: (0,) * a.ndim)
    ospec = pl.BlockSpec((PEER_HEADS, PEER_NKEYS, tm), lambda i: (0, 0, i))
    oshape = lambda dt: jax.ShapeDtypeStruct((PEER_HEADS, PEER_NKEYS, T), dt)
    return pl.pallas_call(
        _route_kernel,
        grid=(T // tm,),
        in_specs=[pl.BlockSpec((D, tm), lambda i: (0, i)), full(wqt), full(skbd)],
        out_specs=[ospec, ospec, ospec, ospec],
        out_shape=[oshape(F32), oshape(F32), oshape(BF16), oshape(BF16)],
        scratch_shapes=[pltpu.VMEM((PEER_HEADS * PEER_DQ, tm), BF16),
                        pltpu.VMEM((PEER_HEADS, 2 * PEER_NKEYS, tm), F32)],
        compiler_params=pltpu.CompilerParams(
            dimension_semantics=("arbitrary",), vmem_limit_bytes=VMEM_LIMIT),
        name="route",
    )(h1t, wqt, skbd)


EXP_TM = 512
EXP_TE = 2048
EXP_IA = EXP_TE // PEER_NKEYS
BF16_ROWS = 16
SQRT_HALF = 0.7071067811865476


def _expert_kernel(xt_ref, h1_ref, u_ref, v_ref, na_ref, e1_ref, rb_ref, e2_ref, g2_ref, b2_ref,
                   o_ref, acc_ref, hu_ref, a_ref):
    j = pl.program_id(1)
    nj = pl.num_programs(1)
    tm = xt_ref.shape[1]
    nsub = PEER_NKEYS // BF16_ROWS

    @pl.when(j == 0)
    def _():
        acc_ref[...] = jnp.zeros_like(acc_ref)

    hu_ref[...] = _dot(u_ref[...], xt_ref[...])
    rows = pl.ds(pl.multiple_of(j * EXP_IA, EXP_IA), EXP_IA)
    for il in range(EXP_IA):
        sl = slice(il * PEER_NKEYS, (il + 1) * PEER_NKEYS)
        hu = hu_ref[sl, :]
        hb = hu.astype(BF16)
        act = hb * (1.0 + lax.erf(hb * SQRT_HALF))
        gate = jnp.zeros((nsub, BF16_ROWS, tm), BF16)
        for h in range(PEER_HEADS):
            na = jnp.broadcast_to(na_ref[h, rows, :][il:il + 1], (BF16_ROWS, tm)).astype(BF16)
            e1 = jnp.broadcast_to(e1_ref[h, rows, :][il:il + 1], (BF16_ROWS, tm)).astype(BF16)
            rb = rb_ref[h].reshape(nsub, BF16_ROWS, tm)
            w = e2_ref[h].reshape(nsub, BF16_ROWS, tm) * e1[None]
            gate = gate + jnp.where(rb < na[None], w, jnp.zeros_like(w))
        a_ref[sl, :] = act * gate.reshape(PEER_NKEYS, tm)
    acc_ref[...] += _dot_tn(a_ref[...], v_ref[...])

    @pl.when(j == nj - 1)
    def _():
        o_ref[...] = _ln(DEEPNORM_ALPHA * h1_ref[...] + acc_ref[...], g2_ref[...], b2_ref[...])


def _experts(h1t, h1, u, v, na, e1, rb, e2, g2, b2):
    T, D = h1.shape
    E = u.shape[0]
    tm, te = EXP_TM, EXP_TE
    gspec = pl.BlockSpec((PEER_HEADS, PEER_NKEYS, tm), lambda i, j: (0, 0, i))
    vec = pl.BlockSpec((1, D), lambda i, j: (0, 0))
    return pl.pallas_call(
        _expert_kernel,
        grid=(T // tm, E // te),
        in_specs=[
            pl.BlockSpec((D, tm), lambda i, j: (0, i)),
            pl.BlockSpec((tm, D), lambda i, j: (i, 0)),
            pl.BlockSpec((te, D), lambda i, j: (j, 0)),
            pl.BlockSpec((te, D), lambda i, j: (j, 0)),
            gspec, gspec, gspec, gspec, vec, vec,
        ],
        out_specs=pl.BlockSpec((tm, D), lambda i, j: (i, 0)),
        out_shape=jax.ShapeDtypeStruct((T, D), F32),
        scratch_shapes=[pltpu.VMEM((tm, D), F32), pltpu.VMEM((te, tm), F32), pltpu.VMEM((te, tm), BF16)],
        compiler_params=pltpu.CompilerParams(
            dimension_semantics=("arbitrary", "arbitrary"), vmem_limit_bytes=VMEM_LIMIT),
        name="experts",
    )(h1t, h1, u, v, na, e1, rb, e2, g2, b2)


def kernel(x, meta_tokens, emb_ln_g, emb_ln_b, rel_bias, w_in, b_in, attn_sinks, w_pa, w_pb, w_o,
           ln1_g, ln1_b, peer_wq, peer_subkeys, peer_u, peer_v, ln2_g, ln2_b):
    B, S, D = x.shape
    T = B * S
    row = lambda a: a.reshape(1, -1).astype(F32)

    w, bi = w_in[0], b_in[0]
    sizes = (W_QA, W_KA, W_KA, W_B, W_B, W_B, N_HEADS_B, D, D)
    offs = np.cumsum((0,) + sizes)
    col = lambda a, k: a[..., offs[k]:offs[k + 1]]
    scale = HEAD_DIM ** -0.5 * LOG2E
    wq = jnp.concatenate([col(w, 0) * scale, col(w, 3) * scale, col(w, 7), col(w, 8)], axis=-1).astype(BF16)
    bq = jnp.concatenate([col(bi, 0) * scale, col(bi, 3) * scale, col(bi, 7), col(bi, 8)], axis=-1)
    dup = lambda a: jnp.repeat(a.reshape(a.shape[:-1] + (N_KV_A, 1, HEAD_DIM)), 2, axis=-2).reshape(
        a.shape[:-1] + (2 * W_KA,))
    wkv = jnp.concatenate([dup(col(w, 1)), dup(col(w, 2)), col(w, 4)], axis=-1).astype(BF16)
    bkv = jnp.concatenate([dup(col(bi, 1)), dup(col(bi, 2)), col(bi, 4)], axis=-1)
    wvbt = col(w, 5).T.astype(BF16)
    bvbt = col(bi, 5).reshape(-1, 1).astype(F32)
    fpad = LANES - N_HEADS_B
    wfb = jnp.pad(col(w, 6), ((0, 0), (0, fpad))).astype(BF16)
    bfb = jnp.pad(col(bi, 6), (0, fpad))

    g0, b0 = row(emb_ln_g), row(emb_ln_b)
    kva, kb, vbt, c, zq = _inproj(x, meta_tokens.astype(F32), g0, b0, wkv, row(bkv), wvbt, bvbt,
                                  wfb, row(bfb), wq, row(bq))
    oa = _swa(zq, kva, rel_bias.astype(F32), attn_sinks[0].astype(F32))
    ob = _fox(zq, kb, vbt, c)

    h1, h1t = _merge(x.reshape(T, D), oa.reshape(T, W_QA), ob.reshape(T, W_B), zq.reshape(T, -1),
                     g0, b0, w_pa[0].astype(BF16), w_pb[0].astype(BF16), w_o[0].astype(BF16),
                     row(ln1_g[0]), row(ln1_b[0]))

    wqt = peer_wq[0].T.astype(BF16)
    sk = peer_subkeys[0].astype(BF16)
    zero = jnp.zeros_like(sk[:, 0])
    skbd = jnp.concatenate([jnp.concatenate([sk[:, 0], zero], axis=-1),
                            jnp.concatenate([zero, sk[:, 1]], axis=-1)], axis=1)
    na, e1, rb, e2 = _route(h1t, wqt, skbd)
    out = _experts(h1t, h1, peer_u[0].astype(BF16), peer_v[0].astype(BF16), na, e1, rb, e2,
                   row(ln2_g[0]), row(ln2_b[0]))
    return out.reshape(B, S, D)
```

```python
import functools
import math

import jax
import jax.numpy as jnp
import numpy as np
from jax import lax
from jax.experimental import pallas as pl
from jax.experimental.pallas import tpu as pltpu

N_META = 16
BLOCK = 128
PAD = BLOCK - N_META
HEAD_DIM = 64
N_HEADS_A = 8
N_KV_A = 2
GROUP_A = N_HEADS_A // N_KV_A
WINDOW = 128
N_HEADS_B = 8
T5_BUCKETS = 32
T5_MAX_DIST = 128
PEER_HEADS = 8
PEER_NKEYS = 128
PEER_DQ = 256
PEER_TOPK = 16
LN_EPS = 1e-5
NEG = -1e30
LOG2E = 1.4426950408889634
DEPTH = 1
DEEPNORM_ALPHA = (2 * DEPTH) ** 0.25

W_QA = N_HEADS_A * HEAD_DIM
W_KA = N_KV_A * HEAD_DIM
W_B = N_HEADS_B * HEAD_DIM
KVA_W = 4 * W_KA

F32 = jnp.float32
BF16 = jnp.bfloat16

VMEM_LIMIT = 56 * 1024 * 1024
VMEM_LIMIT_INPROJ = 60 * 1024 * 1024


def _ln(x, g, b):
    mu = jnp.mean(x, axis=-1, keepdims=True)
    xc = x - mu
    var = jnp.mean(xc * xc, axis=-1, keepdims=True)
    return xc * lax.rsqrt(var + LN_EPS) * g + b


def _dot(a, b):
    return jnp.dot(a, b, preferred_element_type=F32)


def _dot_nt(a, b):
    return lax.dot_general(a, b, (((1,), (1,)), ((), ())), preferred_element_type=F32)


def _dot_tn(a, b):
    return lax.dot_general(a, b, (((0,), (0,)), ((), ())), preferred_element_type=F32)


def _split3(x):
    hi = x.astype(BF16)
    r = x - hi.astype(F32)
    mid = r.astype(BF16)
    lo = (r - mid.astype(F32)).astype(BF16)
    return hi, mid, lo


QT_COLS = 512


def _inproj_kernel(x_ref, meta_ref, g0_ref, b0_ref, wkv_ref, bkv_ref, wvbt_ref, bvbt_ref,
                   wfb_ref, bfb_ref, wq_ref, bq_ref,
                   kva_ref, kb_ref, vbt_ref, c_ref, zq_ref, hs_ref):
    j = pl.program_id(1)
    P = hs_ref.shape[0]

    @pl.when(j == 0)
    def _():
        g0 = g0_ref[...]
        b0 = b0_ref[...]
        hs_ref[0:PAD, :] = jnp.zeros((PAD, hs_ref.shape[1]), BF16)
        hs_ref[PAD:BLOCK, :] = _ln(meta_ref[...], g0, b0).astype(BF16)
        hs_ref[BLOCK:, :] = _ln(x_ref[0], g0, b0).astype(BF16)
        hs = hs_ref[...]
        nkv = KVA_W
        kva_ref[0] = (_dot(hs, wkv_ref[:, :nkv]) + bkv_ref[:, :nkv]).astype(BF16)
        kb_ref[0, :P, :] = (_dot(hs, wkv_ref[:, nkv:]) + bkv_ref[:, nkv:]).astype(BF16)
        kb_ref[0, P:, :] = jnp.zeros((kb_ref.shape[1] - P, kb_ref.shape[2]), BF16)
        vbt_ref[0, :, :P] = (_dot_nt(wvbt_ref[...], hs) + bvbt_ref[...]).astype(BF16)
        vbt_ref[0, :, P:] = jnp.zeros((vbt_ref.shape[1], vbt_ref.shape[2] - P), BF16)
        c_ref[0, P:, :] = jnp.full((c_ref.shape[1] - P, c_ref.shape[2]), C_MASKED, F32)
        lf = jax.nn.log_sigmoid(_dot(hs, wfb_ref[...]) + bfb_ref[...])
        pos = lax.broadcasted_iota(jnp.int32, lf.shape, 0)
        lf = jnp.where(pos >= PAD, lf, 0.0)
        ri = lax.broadcasted_iota(jnp.int32, (BLOCK, BLOCK), 0)
        ci = lax.broadcasted_iota(jnp.int32, (BLOCK, BLOCK), 1)
        tri = jnp.where(ci <= ri, 1.0, 0.0).astype(BF16)
        carry = jnp.zeros((1, lf.shape[1]), F32)
        for n in range(P // BLOCK):
            hi, mid, lo = _split3(lf[n * BLOCK:(n + 1) * BLOCK, :])
            cb = _dot(tri, hi) + _dot(tri, mid) + _dot(tri, lo) + carry
            carry = cb[BLOCK - 1:BLOCK, :]
            if n == 0:
                cb = jnp.where(lax.broadcasted_iota(jnp.int32, cb.shape, 0) >= PAD, cb, C_MASKED)
            c_ref[0, n * BLOCK:(n + 1) * BLOCK, :] = cb

    @pl.when(j > 0)
    def _():
        zq_ref[0] = (_dot(hs_ref[BLOCK:, :], wq_ref[...]) + bq_ref[...]).astype(BF16)


def _inproj(x, meta, g0, b0, wkv, bkv, wvbt, bvbt, wfb, bfb, wq, bq):
    B, S, D = x.shape
    P = S + BLOCK
    P2 = -(-P // FOX_TK) * FOX_TK
    nq = wq.shape[1] // QT_COLS
    full = lambda shape: pl.BlockSpec(shape, lambda b, j: (0,) * len(shape))
    qcol = lambda b, j: (0, jnp.maximum(j - 1, 0))
    return pl.pallas_call(
        _inproj_kernel,
        grid=(B, nq + 1),
        in_specs=[
            pl.BlockSpec((1, S, D), lambda b, j: (b, 0, 0)),
            full(meta.shape), full(g0.shape), full(b0.shape),
            full(wkv.shape), full(bkv.shape), full(wvbt.shape), full(bvbt.shape),
            full(wfb.shape), full(bfb.shape),
            pl.BlockSpec((D, QT_COLS), qcol),
            pl.BlockSpec((1, QT_COLS), qcol),
        ],
        out_specs=[
            pl.BlockSpec((1, P, KVA_W), lambda b, j: (b, 0, 0)),
            pl.BlockSpec((1, P2, W_B), lambda b, j: (b, 0, 0)),
            pl.BlockSpec((1, W_B, P2), lambda b, j: (b, 0, 0)),
            pl.BlockSpec((1, P2, LANES), lambda b, j: (b, 0, 0)),
            pl.BlockSpec((1, S, QT_COLS), lambda b, j: (b, 0, jnp.maximum(j - 1, 0))),
        ],
        out_shape=[
            jax.ShapeDtypeStruct((B, P, KVA_W), BF16),
            jax.ShapeDtypeStruct((B, P2, W_B), BF16),
            jax.ShapeDtypeStruct((B, W_B, P2), BF16),
            jax.ShapeDtypeStruct((B, P2, LANES), F32),
            jax.ShapeDtypeStruct((B, S, wq.shape[1]), BF16),
        ],
        scratch_shapes=[pltpu.VMEM((P, D), BF16)],
        compiler_params=pltpu.CompilerParams(
            dimension_semantics=("arbitrary", "arbitrary"), vmem_limit_bytes=VMEM_LIMIT_INPROJ),
        name="inproj",
    )(x, meta, g0, b0, wkv, bkv, wvbt, bvbt, wfb, bfb, wq, bq)


def _t5_bucket_table():
    qi = np.arange(BLOCK)[:, None] + BLOCK
    kj = np.arange(2 * BLOCK)[None, :]
    n = np.maximum(qi - kj, 0)
    max_exact = T5_BUCKETS // 2
    nf = np.maximum(n, 1).astype(np.float32)
    large = max_exact + (np.log(nf / np.float32(max_exact)) / np.float32(math.log(T5_MAX_DIST / max_exact))
                         * (T5_BUCKETS - max_exact)).astype(np.int32)
    large = np.minimum(large, T5_BUCKETS - 1)
    bucket = np.where(n < max_exact, n, large).astype(np.int32)
    dist = qi - kj
    window = (dist >= 0) & (dist < WINDOW)
    return np.where(window, bucket, -1).astype(np.int32)


def _swa_kernel(bucket_ref, rel_ref, sink_ref, q_ref, kvp_ref, kvc_ref, o_ref, bias_ref):
    b = pl.program_id(0)
    n = pl.program_id(1)

    @pl.when((b == 0) & (n == 0))
    def _():
        bucket = bucket_ref[...]
        for h in range(N_HEADS_A):
            acc = jnp.where(bucket < 0, NEG, 0.0).astype(F32)
            for t in range(T5_BUCKETS):
                acc = jnp.where(bucket == t, rel_ref[t, h] * LOG2E, acc)
            bias_ref[h] = acc

    kcol = lax.broadcasted_iota(jnp.int32, (BLOCK, 2 * BLOCK), 1)
    kabs_ok = (n * BLOCK + kcol) >= PAD
    pw = 2 * HEAD_DIM
    first = lax.broadcasted_iota(jnp.int32, (BLOCK, pw), 1) < HEAD_DIM
    vfirst = lax.broadcasted_iota(jnp.int32, (2 * BLOCK, pw), 1) < HEAD_DIM
    for g in range(N_KV_A):
        kk = jnp.concatenate([kvp_ref[0, :, g * pw:(g + 1) * pw],
                              kvc_ref[0, :, g * pw:(g + 1) * pw]], axis=0)
        vv = jnp.concatenate([kvp_ref[0, :, (N_KV_A + g) * pw:(N_KV_A + g + 1) * pw],
                              kvc_ref[0, :, (N_KV_A + g) * pw:(N_KV_A + g + 1) * pw]], axis=0)
        vzero = jnp.zeros_like(vv)
        vm = (jnp.where(vfirst, vv, vzero), jnp.where(vfirst, vzero, vv))
        for pp in range(GROUP_A // 2):
            pair = g * (GROUP_A // 2) + pp
            q = q_ref[0, :, pair * pw:(pair + 1) * pw]
            qzero = jnp.zeros_like(q)
            qm = (jnp.where(first, q, qzero), jnp.where(first, qzero, q))
            out = None
            for i in range(2):
                h = 2 * pair + i
                s = _dot_nt(qm[i], kk) + bias_ref[h]
                s = jnp.where(kabs_ok, s, NEG)
                sink = sink_ref[h] * LOG2E
                m = jnp.maximum(jnp.max(s, axis=-1, keepdims=True), sink)
                p = jnp.exp2(s - m)
                denom = jnp.sum(p, axis=-1, keepdims=True) + jnp.exp2(sink - m)
                o = _dot(p.astype(BF16), vm[i]) / denom
                out = o if out is None else out + o
            o_ref[0, :, pair * pw:(pair + 1) * pw] = out.astype(BF16)


def _swa(zq, kva, rel_bias, sinks):
    B, S, _ = zq.shape
    nb = S // BLOCK
    bucket = jnp.asarray(_t5_bucket_table())
    smem = pl.BlockSpec(memory_space=pltpu.SMEM)
    return pl.pallas_call(
        _swa_kernel,
        grid=(B, nb),
        in_specs=[
            pl.BlockSpec((BLOCK, 2 * BLOCK), lambda b, n: (0, 0)),
            smem, smem,
            pl.BlockSpec((1, BLOCK, W_QA), lambda b, n: (b, n, 0)),
            pl.BlockSpec((1, BLOCK, KVA_W), lambda b, n: (b, n, 0)),
            pl.BlockSpec((1, BLOCK, KVA_W), lambda b, n: (b, n + 1, 0)),
        ],
        out_specs=pl.BlockSpec((1, BLOCK, W_QA), lambda b, n: (b, n, 0)),
        out_shape=jax.ShapeDtypeStruct((B, S, W_QA), BF16),
        scratch_shapes=[pltpu.VMEM((N_HEADS_A, BLOCK, 2 * BLOCK), F32)],
        compiler_params=pltpu.CompilerParams(
            dimension_semantics=("arbitrary", "arbitrary"), vmem_limit_bytes=VMEM_LIMIT),
        name="swa",
    )(bucket, rel_bias, sinks, zq, kva, kva)


FOX_TQ = 512
FOX_TK = 512
FOX_PAIRS = 4
C_MASKED = 1e30


def _fox_kernel(q_ref, k_ref, vt_ref, c_ref, o_ref):
    grp = pl.program_id(1)
    qi = pl.program_id(2)
    tq = q_ref.shape[1]
    tk = FOX_TK
    pw = 2 * HEAD_DIM
    q0 = BLOCK + qi * tq
    first = lax.broadcasted_iota(jnp.int32, (tq, pw), 1) < HEAD_DIM
    qm = []
    for pr in range(FOX_PAIRS):
        q = q_ref[0, :, pr * pw:(pr + 1) * pw]
        qzero = jnp.zeros_like(q)
        qm.append((jnp.where(first, q, qzero), jnp.where(first, qzero, q)))
    kpos = lax.broadcasted_iota(jnp.int32, (tk, tq), 0)
    qpos = q0 + lax.broadcasted_iota(jnp.int32, (tk, tq), 1)
    top = lax.broadcasted_iota(jnp.int32, (pw, tq), 0) < HEAD_DIM

    def ccols(start, size):
        c = c_ref[0, pl.ds(start, size), :]
        head = lax.broadcasted_iota(jnp.int32, c.shape, 1)
        return [[jnp.sum(jnp.where(head == 2 * (grp * FOX_PAIRS + pr) + i, c, 0.0), axis=1, keepdims=True)
                 for i in range(2)] for pr in range(FOX_PAIRS)]

    cq0 = [[col[0:1, :] for col in cols] for cols in ccols(pl.multiple_of(q0, BLOCK), 8)]

    def chunk(jc, carry, causal):
        ks = pl.multiple_of(jc * tk, tk)
        cc = ccols(ks, tk)
        out = []
        for pr in range(FOX_PAIRS):
            m, l, acc = carry[pr]
            kk = k_ref[0, pl.ds(ks, tk), pr * pw:(pr + 1) * pw]
            vt = vt_ref[0, pr * pw:(pr + 1) * pw, pl.ds(ks, tk)]
            vtop = lax.broadcasted_iota(jnp.int32, vt.shape, 0) < HEAD_DIM
            vzero = jnp.zeros_like(vt)
            vtm = (jnp.where(vtop, vt, vzero), jnp.where(vtop, vzero, vt))
            m_out, l_out, alphas, pv = [], [], [], None
            for i in range(2):
                s = _dot_nt(kk, qm[pr][i]) + (cq0[pr][i] - cc[pr][i]) * LOG2E
                if causal:
                    s = jnp.where(ks + kpos <= qpos, s, NEG)
                m_new = jnp.maximum(m[i], jnp.max(s, axis=0, keepdims=True))
                alpha = jnp.exp2(m[i] - m_new)
                p = jnp.exp2(s - m_new)
                m_out.append(m_new)
                l_out.append(alpha * l[i] + jnp.sum(p, axis=0, keepdims=True))
                alphas.append(alpha)
                d = _dot(vtm[i], p.astype(BF16))
                pv = d if pv is None else pv + d
            acc = jnp.where(top, alphas[0], alphas[1]) * acc + pv
            out.append((tuple(m_out), tuple(l_out), acc))
        return tuple(out)

    n_full = q0 // tk
    n_chunks = (q0 + tq + tk - 1) // tk
    neg = jnp.full((1, tq), NEG, F32)
    zero = jnp.zeros((1, tq), F32)
    carry = tuple(((neg, neg), (zero, zero), jnp.zeros((pw, tq), F32)) for _ in range(FOX_PAIRS))
    carry = lax.fori_loop(0, n_full, functools.partial(chunk, causal=False), carry)
    carry = lax.fori_loop(n_full, n_chunks, functools.partial(chunk, causal=True), carry)
    for pr in range(FOX_PAIRS):
        m, l, acc = carry[pr]
        o_ref[0, :, pr * pw:(pr + 1) * pw] = (acc / jnp.where(top, l[0], l[1])).T.astype(BF16)


def _fox(zq, kb, vbt, c):
    B, S, _ = zq.shape
    P = kb.shape[1]
    gw = FOX_PAIRS * 2 * HEAD_DIM
    ngrp = W_B // gw
    qblk0 = W_QA // gw
    return pl.pallas_call(
        _fox_kernel,
        grid=(B, ngrp, S // FOX_TQ),
        in_specs=[
            pl.BlockSpec((1, FOX_TQ, gw), lambda b, p, i: (b, i, qblk0 + p)),
            pl.BlockSpec((1, P, gw), lambda b, p, i: (b, 0, p)),
            pl.BlockSpec((1, gw, P), lambda b, p, i: (b, p, 0)),
            pl.BlockSpec((1, P, LANES), lambda b, p, i: (b, 0, 0)),
        ],
        out_specs=pl.BlockSpec((1, FOX_TQ, gw), lambda b, p, i: (b, i, p)),
        out_shape=jax.ShapeDtypeStruct((B, S, W_B), BF16),
        compiler_params=pltpu.CompilerParams(
            dimension_semantics=("arbitrary", "arbitrary", "arbitrary"), vmem_limit_bytes=VMEM_LIMIT),
        name="fox",
    )(zq, kb, vbt, c)


MERGE_TM = 512


def _merge_kernel(x_ref, oa_ref, ob_ref, ga_ref, gb_ref, g0_ref, b0_ref, wpa_ref, wpb_ref, wo_ref,
                  g1_ref, b1_ref, h1_ref, h1t_ref):
    h = _ln(x_ref[...], g0_ref[...], b0_ref[...])
    ya = _dot(oa_ref[...], wpa_ref[...])
    yb = _dot(ob_ref[...], wpb_ref[...])
    merged = jax.nn.sigmoid(ga_ref[...].astype(F32)) * ya + jax.nn.sigmoid(gb_ref[...].astype(F32)) * yb
    y = _dot(merged.astype(BF16), wo_ref[...])
    h1 = _ln(DEEPNORM_ALPHA * h + y, g1_ref[...], b1_ref[...])
    h1_ref[...] = h1
    h1t_ref[...] = h1.T.astype(BF16)


def _merge(x2, oa2, ob2, zq2, g0, b0, wpa, wpb, wo, g1, b1):
    T, D = x2.shape
    tm = MERGE_TM
    full = lambda a: pl.BlockSpec(a.shape, lambda i: (0,) * a.ndim)
    gate0 = (W_QA + W_B) // D
    return pl.pallas_call(
        _merge_kernel,
        grid=(T // tm,),
        in_specs=[
            pl.BlockSpec((tm, D), lambda i: (i, 0)),
            pl.BlockSpec((tm, W_QA), lambda i: (i, 0)),
            pl.BlockSpec((tm, W_B), lambda i: (i, 0)),
            pl.BlockSpec((tm, D), lambda i: (i, gate0)),
            pl.BlockSpec((tm, D), lambda i: (i, gate0 + 1)),
            full(g0), full(b0), full(wpa), full(wpb), full(wo), full(g1), full(b1),
        ],
        out_specs=[pl.BlockSpec((tm, D), lambda i: (i, 0)), pl.BlockSpec((D, tm), lambda i: (0, i))],
        out_shape=[jax.ShapeDtypeStruct((T, D), F32), jax.ShapeDtypeStruct((D, T), BF16)],
        compiler_params=pltpu.CompilerParams(
            dimension_semantics=("arbitrary",), vmem_limit_bytes=VMEM_LIMIT),
        name="merge",
    )(x2, oa2, ob2, zq2, zq2, g0, b0, wpa, wpb, wo, g1, b1)


ROUTE_TM = 512
ROUTE_UNROLL = 2
LANES = 128
HALF_DQ = PEER_DQ // 2
RANK_OUT = 64.0


def _sort16_network():
    pairs = []

    def merge(lo, n, r):
        step = 2 * r
        if step < n:
            merge(lo, n, step)
            merge(lo + r, n, step)
            pairs.extend((i, i + r) for i in range(lo + r, lo + n - r, step))
        else:
            pairs.append((lo, lo + r))

    def sort(lo, n):
        if n > 1:
            sort(lo, n // 2)
            sort(lo + n // 2, n // 2)
            merge(lo, n, 1)

    sort(0, 16)
    return tuple(pairs)


def _top_values(s):
    rows = s.shape[0] // 16
    v = [s[i * rows:(i + 1) * rows] for i in range(16)]
    for i, j in _sort16_network():
        v[i], v[j] = jnp.maximum(v[i], v[j]), jnp.minimum(v[i], v[j])
    tops = []
    for r in range(PEER_TOPK):
        m = jnp.max(v[0], axis=0, keepdims=True)
        tops.append(m)
        if r + 1 < PEER_TOPK:
            drawn = v[0] == m
            for k in range(PEER_TOPK - 1 - r):
                v[k] = jnp.where(drawn, v[k + 1], v[k])
    return tops


def _stack_rows(rows):
    n = len(rows)
    ri = lax.broadcasted_iota(jnp.int32, (n, rows[0].shape[1]), 0)
    out = jnp.broadcast_to(rows[0], ri.shape)
    for r in range(1, n):
        out = jnp.where(ri == r, rows[r], out)
    return out


def _route_kernel(ht_ref, wqt_ref, sk_ref, na_ref, e1_ref, rb_ref, e2_ref, q_ref, sc_ref):
    K = PEER_TOPK
    q_ref[...] = _dot(wqt_ref[...], ht_ref[...]).astype(BF16)
    for h in range(PEER_HEADS):
        sc_ref[h] = _dot(sk_ref[h], q_ref[h * PEER_DQ:(h + 1) * PEER_DQ, :])

    n_groups = ht_ref.shape[1] // LANES

    def one_group(h, g):
        lanes = pl.ds(pl.multiple_of(g * LANES, LANES), LANES)
        s1 = sc_ref[h, :PEER_NKEYS, lanes]
        s2 = sc_ref[h, PEER_NKEYS:, lanes]
        a = _top_values(s1)
        b = _top_values(s2)
        rb = jnp.full(s2.shape, RANK_OUT, F32)
        for r in range(K):
            rb = jnp.where(s2 == b[r], float(r), rb)
        bmat16 = _stack_rows(b)
        bmat8 = bmat16[:8]
        cands = [a[0] + bmat16]
        for i in range(1, 8):
            cands.append(a[i] + bmat8)
        cands.append(_stack_rows(a[8:]) + b[0])
        cand = jnp.concatenate(cands, axis=0)
        vmax = a[0] + b[0]
        z = jnp.zeros_like(vmax)
        for r in range(K):
            m = jnp.max(cand, axis=0, keepdims=True)
            cand = jnp.where(cand == m, -jnp.inf, cand)
            z = z + jnp.exp(m - vmax)
        sel = jnp.where(cand == -jnp.inf, 1.0, 0.0)
        counts = [jnp.sum(sel[0:16], axis=0, keepdims=True)]
        for i in range(1, 8):
            counts.append(jnp.sum(sel[8 + 8 * i:16 + 8 * i], axis=0, keepdims=True))
        n_tail = jnp.sum(sel[72:80], axis=0, keepdims=True)
        low = jnp.full_like(n_tail, jnp.inf)
        for k in range(1, 9):
            low = jnp.where(n_tail == float(k), a[7 + k], low)
        na = jnp.where(s1 >= low, jnp.where(s1 < a[7], 1.0, 0.0), 0.0)
        for i in range(8):
            na = jnp.where(s1 == a[i], counts[i], na)
        na_ref[h, :, lanes] = na
        e1_ref[h, :, lanes] = jnp.exp(s1 - a[0])
        rb_ref[h, :, lanes] = rb.astype(BF16)
        e2_ref[h, :, lanes] = (jnp.exp(s2 - b[0]) * (0.5 / z)).astype(BF16)

    per_head = n_groups // ROUTE_UNROLL

    def group(idx, _):
        h = idx // per_head
        g0 = (idx % per_head) * ROUTE_UNROLL
        for k in range(ROUTE_UNROLL):
            one_group(h, g0 + k)
        return 0

    lax.fori_loop(0, PEER_HEADS * per_head, group, 0)


def _route(h1t, wqt, skbd):
    D, T = h1t.shape
    tm = ROUTE_TM
    full = lambda a: pl.BlockSpec(a.shape, lambda i: (0,) * a.ndim)
    ospec = pl.BlockSpec((PEER_HEADS, PEER_NKEYS, tm), lambda i: (0, 0, i))
    oshape = lambda dt: jax.ShapeDtypeStruct((PEER_HEADS, PEER_NKEYS, T), dt)
    return pl.pallas_call(
        _route_kernel,
        grid=(T // tm,),
        in_specs=[pl.BlockSpec((D, tm), lambda i: (0, i)), full(wqt), full(skbd)],
        out_specs=[ospec, ospec, ospec, ospec],
        out_shape=[oshape(F32), oshape(F32), oshape(BF16), oshape(BF16)],
        scratch_shapes=[pltpu.VMEM((PEER_HEADS * PEER_DQ, tm), BF16),
                        pltpu.VMEM((PEER_HEADS, 2 * PEER_NKEYS, tm), F32)],
        compiler_params=pltpu.CompilerParams(
            dimension_semantics=("arbitrary",), vmem_limit_bytes=VMEM_LIMIT),
        name="route",
    )(h1t, wqt, skbd)


EXP_TM = 512
EXP_TE = 2048
EXP_IA = EXP_TE // PEER_NKEYS
BF16_ROWS = 16
SQRT_HALF = 0.7071067811865476


def _expert_kernel(xt_ref, h1_ref, u_ref, v_ref, na_ref, e1_ref, rb_ref, e2_ref, g2_ref, b2_ref,
                   o_ref, acc_ref, hu_ref, a_ref):
    j = pl.program_id(1)
    nj = pl.num_programs(1)
    tm = xt_ref.shape[1]
    nsub = PEER_NKEYS // BF16_ROWS

    @pl.when(j == 0)
    def _():
        acc_ref[...] = jnp.zeros_like(acc_ref)

    hu_ref[...] = _dot(u_ref[...], xt_ref[...])
    rows = pl.ds(pl.multiple_of(j * EXP_IA, EXP_IA), EXP_IA)
    for il in range(EXP_IA):
        sl = slice(il * PEER_NKEYS, (il + 1) * PEER_NKEYS)
        hu = hu_ref[sl, :]
        hb = hu.astype(BF16)
        act = hb * (1.0 + lax.erf(hb * SQRT_HALF))
        gate = jnp.zeros((nsub, BF16_ROWS, tm), BF16)
        for h in range(PEER_HEADS):
            na = jnp.broadcast_to(na_ref[h, rows, :][il:il + 1], (BF16_ROWS, tm)).astype(BF16)
            e1 = jnp.broadcast_to(e1_ref[h, rows, :][il:il + 1], (BF16_ROWS, tm)).astype(BF16)
            rb = rb_ref[h].reshape(nsub, BF16_ROWS, tm)
            w = e2_ref[h].reshape(nsub, BF16_ROWS, tm) * e1[None]
            gate = gate + jnp.where(rb < na[None], w, jnp.zeros_like(w))
        a_ref[sl, :] = act * gate.reshape(PEER_NKEYS, tm)
    acc_ref[...] += _dot_tn(a_ref[...], v_ref[...])

    @pl.when(j == nj - 1)
    def _():
        o_ref[...] = _ln(DEEPNORM_ALPHA * h1_ref[...] + acc_ref[...], g2_ref[...], b2_ref[...])


def _experts(h1t, h1, u, v, na, e1, rb, e2, g2, b2):
    T, D = h1.shape
    E = u.shape[0]
    tm, te = EXP_TM, EXP_TE
    gspec = pl.BlockSpec((PEER_HEADS, PEER_NKEYS, tm), lambda i, j: (0, 0, i))
    vec = pl.BlockSpec((1, D), lambda i, j: (0, 0))
    return pl.pallas_call(
        _expert_kernel,
        grid=(T // tm, E // te),
        in_specs=[
            pl.BlockSpec((D, tm), lambda i, j: (0, i)),
            pl.BlockSpec((tm, D), lambda i, j: (i, 0)),
            pl.BlockSpec((te, D), lambda i, j: (j, 0)),
            pl.BlockSpec((te, D), lambda i, j: (j, 0)),
            gspec, gspec, gspec, gspec, vec, vec,
        ],
        out_specs=pl.BlockSpec((tm, D), lambda i, j: (i, 0)),
        out_shape=jax.ShapeDtypeStruct((T, D), F32),
        scratch_shapes=[pltpu.VMEM((tm, D), F32), pltpu.VMEM((te, tm), F32), pltpu.VMEM((te, tm), BF16)],
        compiler_params=pltpu.CompilerParams(
            dimension_semantics=("arbitrary", "arbitrary"), vmem_limit_bytes=VMEM_LIMIT),
        name="experts",
    )(h1t, h1, u, v, na, e1, rb, e2, g2, b2)


def kernel(x, meta_tokens, emb_ln_g, emb_ln_b, rel_bias, w_in, b_in, attn_sinks, w_pa, w_pb, w_o,
           ln1_g, ln1_b, peer_wq, peer_subkeys, peer_u, peer_v, ln2_g, ln2_b):
    B, S, D = x.shape
    T = B * S
    row = lambda a: a.reshape(1, -1).astype(F32)

    w, bi = w_in[0], b_in[0]
    sizes = (W_QA, W_KA, W_KA, W_B, W_B, W_B, N_HEADS_B, D, D)
    offs = np.cumsum((0,) + sizes)
    col = lambda a, k: a[..., offs[k]:offs[k + 1]]
    scale = HEAD_DIM ** -0.5 * LOG2E
    wq = jnp.concatenate([col(w, 0) * scale, col(w, 3) * scale, col(w, 7), col(w, 8)], axis=-1).astype(BF16)
    bq = jnp.concatenate([col(bi, 0) * scale, col(bi, 3) * scale, col(bi, 7), col(bi, 8)], axis=-1)
    dup = lambda a: jnp.repeat(a.reshape(a.shape[:-1] + (N_KV_A, 1, HEAD_DIM)), 2, axis=-2).reshape(
        a.shape[:-1] + (2 * W_KA,))
    wkv = jnp.concatenate([dup(col(w, 1)), dup(col(w, 2)), col(w, 4)], axis=-1).astype(BF16)
    bkv = jnp.concatenate([dup(col(bi, 1)), dup(col(bi, 2)), col(bi, 4)], axis=-1)
    wvbt = col(w, 5).T.astype(BF16)
    bvbt = col(bi, 5).reshape(-1, 1).astype(F32)
    fpad = LANES - N_HEADS_B
    wfb = jnp.pad(col(w, 6), ((0, 0), (0, fpad))).astype(BF16)
    bfb = jnp.pad(col(bi, 6), (0, fpad))

    g0, b0 = row(emb_ln_g), row(emb_ln_b)
    kva, kb, vbt, c, zq = _inproj(x, meta_tokens.astype(F32), g0, b0, wkv, row(bkv), wvbt, bvbt,
                                  wfb, row(bfb), wq, row(bq))
    oa = _swa(zq, kva, rel_bias.astype(F32), attn_sinks[0].astype(F32))
    ob = _fox(zq, kb, vbt, c)

    h1, h1t = _merge(x.reshape(T, D), oa.reshape(T, W_QA), ob.reshape(T, W_B), zq.reshape(T, -1),
                     g0, b0, w_pa[0].astype(BF16), w_pb[0].astype(BF16), w_o[0].astype(BF16),
                     row(ln1_g[0]), row(ln1_b[0]))

    wqt = peer_wq[0].T.astype(BF16)
    sk = peer_subkeys[0].astype(BF16)
    zero = jnp.zeros_like(sk[:, 0])
    skbd = jnp.concatenate([jnp.concatenate([sk[:, 0], zero], axis=-1),
                            jnp.concatenate([zero, sk[:, 1]], axis=-1)], axis=1)
    na, e1, rb, e2 = _route(h1t, wqt, skbd)
    out = _experts(h1t, h1, peer_u[0].astype(BF16), peer_v[0].astype(BF16), na, e1, rb, e2,
                   row(ln2_g[0]), row(ln2_b[0]))
    return out.reshape(B, S, D)
```

```python
import functools
import math

import jax
import jax.numpy as jnp
import numpy as np
from jax import lax
from jax.experimental import pallas as pl
from jax.experimental.pallas import tpu as pltpu

N_META = 16
BLOCK = 128
PAD = BLOCK - N_META
HEAD_DIM = 64
N_HEADS_A = 8
N_KV_A = 2
GROUP_A = N_HEADS_A // N_KV_A
WINDOW = 128
N_HEADS_B = 8
T5_BUCKETS = 32
T5_MAX_DIST = 128
PEER_HEADS = 8
PEER_NKEYS = 128
PEER_DQ = 256
PEER_TOPK = 16
LN_EPS = 1e-5
NEG = -1e30
LOG2E = 1.4426950408889634
DEPTH = 1
DEEPNORM_ALPHA = (2 * DEPTH) ** 0.25

W_QA = N_HEADS_A * HEAD_DIM
W_KA = N_KV_A * HEAD_DIM
W_B = N_HEADS_B * HEAD_DIM
KVA_W = 4 * W_KA

F32 = jnp.float32
BF16 = jnp.bfloat16

VMEM_LIMIT = 56 * 1024 * 1024
VMEM_LIMIT_INPROJ = 60 * 1024 * 1024


def _ln(x, g, b):
    mu = jnp.mean(x, axis=-1, keepdims=True)
    xc = x - mu
    var = jnp.mean(xc * xc, axis=-1, keepdims=True)
    return xc * lax.rsqrt(var + LN_EPS) * g + b


def _dot(a, b):
    return jnp.dot(a, b, preferred_element_type=F32)


def _dot_nt(a, b):
    return lax.dot_general(a, b, (((1,), (1,)), ((), ())), preferred_element_type=F32)


def _dot_tn(a, b):
    return lax.dot_general(a, b, (((0,), (0,)), ((), ())), preferred_element_type=F32)


def _split3(x):
    hi = x.astype(BF16)
    r = x - hi.astype(F32)
    mid = r.astype(BF16)
    lo = (r - mid.astype(F32)).astype(BF16)
    return hi, mid, lo


QT_COLS = 512


def _inproj_kernel(x_ref, meta_ref, g0_ref, b0_ref, wkv_ref, bkv_ref, wvbt_ref, bvbt_ref,
                   wfb_ref, bfb_ref, wq_ref, bq_ref,
                   kva_ref, kb_ref, vbt_ref, c_ref, zq_ref, hs_ref):
    j = pl.program_id(1)
    P = hs_ref.shape[0]

    @pl.when(j == 0)
    def _():
        g0 = g0_ref[...]
        b0 = b0_ref[...]
        hs_ref[0:PAD, :] = jnp.zeros((PAD, hs_ref.shape[1]), BF16)
        hs_ref[PAD:BLOCK, :] = _ln(meta_ref[...], g0, b0).astype(BF16)
        hs_ref[BLOCK:, :] = _ln(x_ref[0], g0, b0).astype(BF16)
        hs = hs_ref[...]
        nkv = KVA_W
        kva_ref[0] = (_dot(hs, wkv_ref[:, :nkv]) + bkv_ref[:, :nkv]).astype(BF16)
        kb_ref[0, :P, :] = (_dot(hs, wkv_ref[:, nkv:]) + bkv_ref[:, nkv:]).astype(BF16)
        kb_ref[0, P:, :] = jnp.zeros((kb_ref.shape[1] - P, kb_ref.shape[2]), BF16)
        vbt_ref[0, :, :P] = (_dot_nt(wvbt_ref[...], hs) + bvbt_ref[...]).astype(BF16)
        vbt_ref[0, :, P:] = jnp.zeros((vbt_ref.shape[1], vbt_ref.shape[2] - P), BF16)
        c_ref[0, P:, :] = jnp.full((c_ref.shape[1] - P, c_ref.shape[2]), C_MASKED, F32)
        lf = jax.nn.log_sigmoid(_dot(hs, wfb_ref[...]) + bfb_ref[...])
        pos = lax.broadcasted_iota(jnp.int32, lf.shape, 0)
        lf = jnp.where(pos >= PAD, lf, 0.0)
        ri = lax.broadcasted_iota(jnp.int32, (BLOCK, BLOCK), 0)
        ci = lax.broadcasted_iota(jnp.int32, (BLOCK, BLOCK), 1)
        tri = jnp.where(ci <= ri, 1.0, 0.0).astype(BF16)
        carry = jnp.zeros((1, lf.shape[1]), F32)
        for n in range(P // BLOCK):
            hi, mid, lo = _split3(lf[n * BLOCK:(n + 1) * BLOCK, :])
            cb = _dot(tri, hi) + _dot(tri, mid) + _dot(tri, lo) + carry
            carry = cb[BLOCK - 1:BLOCK, :]
            if n == 0:
                cb = jnp.where(lax.broadcasted_iota(jnp.int32, cb.shape, 0) >= PAD, cb, C_MASKED)
            c_ref[0, n * BLOCK:(n + 1) * BLOCK, :] = cb

    @pl.when(j > 0)
    def _():
        zq_ref[0] = (_dot(hs_ref[BLOCK:, :], wq_ref[...]) + bq_ref[...]).astype(BF16)


def _inproj(x, meta, g0, b0, wkv, bkv, wvbt, bvbt, wfb, bfb, wq, bq):
    B, S, D = x.shape
    P = S + BLOCK
    P2 = -(-P // FOX_TK) * FOX_TK
    nq = wq.shape[1] // QT_COLS
    full = lambda shape: pl.BlockSpec(shape, lambda b, j: (0,) * len(shape))
    qcol = lambda b, j: (0, jnp.maximum(j - 1, 0))
    return pl.pallas_call(
        _inproj_kernel,
        grid=(B, nq + 1),
        in_specs=[
            pl.BlockSpec((1, S, D), lambda b, j: (b, 0, 0)),
            full(meta.shape), full(g0.shape), full(b0.shape),
            full(wkv.shape), full(bkv.shape), full(wvbt.shape), full(bvbt.shape),
            full(wfb.shape), full(bfb.shape),
            pl.BlockSpec((D, QT_COLS), qcol),
            pl.BlockSpec((1, QT_COLS), qcol),
        ],
        out_specs=[
            pl.BlockSpec((1, P, KVA_W), lambda b, j: (b, 0, 0)),
            pl.BlockSpec((1, P2, W_B), lambda b, j: (b, 0, 0)),
            pl.BlockSpec((1, W_B, P2), lambda b, j: (b, 0, 0)),
            pl.BlockSpec((1, P2, LANES), lambda b, j: (b, 0, 0)),
            pl.BlockSpec((1, S, QT_COLS), lambda b, j: (b, 0, jnp.maximum(j - 1, 0))),
        ],
        out_shape=[
            jax.ShapeDtypeStruct((B, P, KVA_W), BF16),
            jax.ShapeDtypeStruct((B, P2, W_B), BF16),
            jax.ShapeDtypeStruct((B, W_B, P2), BF16),
            jax.ShapeDtypeStruct((B, P2, LANES), F32),
            jax.ShapeDtypeStruct((B, S, wq.shape[1]), BF16),
        ],
        scratch_shapes=[pltpu.VMEM((P, D), BF16)],
        compiler_params=pltpu.CompilerParams(
            dimension_semantics=("arbitrary", "arbitrary"), vmem_limit_bytes=VMEM_LIMIT_INPROJ),
        name="inproj",
    )(x, meta, g0, b0, wkv, bkv, wvbt, bvbt, wfb, bfb, wq, bq)


def _t5_bucket_table():
    qi = np.arange(BLOCK)[:, None] + BLOCK
    kj = np.arange(2 * BLOCK)[None, :]
    n = np.maximum(qi - kj, 0)
    max_exact = T5_BUCKETS // 2
    nf = np.maximum(n, 1).astype(np.float32)
    large = max_exact + (np.log(nf / np.float32(max_exact)) / np.float32(math.log(T5_MAX_DIST / max_exact))
                         * (T5_BUCKETS - max_exact)).astype(np.int32)
    large = np.minimum(large, T5_BUCKETS - 1)
    bucket = np.where(n < max_exact, n, large).astype(np.int32)
    dist = qi - kj
    window = (dist >= 0) & (dist < WINDOW)
    return np.where(window, bucket, -1).astype(np.int32)


SWA_QBLOCKS = 4


def _swa_kernel(bucket_ref, rel_ref, sink_ref, q_ref, *rest):
    kv_refs, o_ref, bias_ref = rest[:SWA_QBLOCKS + 1], rest[SWA_QBLOCKS + 1], rest[SWA_QBLOCKS + 2]
    b = pl.program_id(0)
    n = pl.program_id(1)

    @pl.when((b == 0) & (n == 0))
    def _():
        bucket = bucket_ref[...]
        for h in range(N_HEADS_A):
            acc = jnp.where(bucket < 0, NEG, 0.0).astype(F32)
            for t in range(T5_BUCKETS):
                acc = jnp.where(bucket == t, rel_ref[t, h] * LOG2E, acc)
            bias_ref[h] = acc

    kcol = lax.broadcasted_iota(jnp.int32, (BLOCK, 2 * BLOCK), 1)
    pw = 2 * HEAD_DIM
    first = lax.broadcasted_iota(jnp.int32, (BLOCK, pw), 1) < HEAD_DIM
    vfirst = lax.broadcasted_iota(jnp.int32, (2 * BLOCK, pw), 1) < HEAD_DIM
    for sub in range(SWA_QBLOCKS):
        kvp_ref, kvc_ref = kv_refs[sub], kv_refs[sub + 1]
        rows = slice(sub * BLOCK, (sub + 1) * BLOCK)
        kabs_ok = ((n * SWA_QBLOCKS + sub) * BLOCK + kcol) >= PAD
        for g in range(N_KV_A):
            kk = jnp.concatenate([kvp_ref[0, :, g * pw:(g + 1) * pw],
                                  kvc_ref[0, :, g * pw:(g + 1) * pw]], axis=0)
            vv = jnp.concatenate([kvp_ref[0, :, (N_KV_A + g) * pw:(N_KV_A + g + 1) * pw],
                                  kvc_ref[0, :, (N_KV_A + g) * pw:(N_KV_A + g + 1) * pw]], axis=0)
            vzero = jnp.zeros_like(vv)
            vm = (jnp.where(vfirst, vv, vzero), jnp.where(vfirst, vzero, vv))
            for pp in range(GROUP_A // 2):
                pair = g * (GROUP_A // 2) + pp
                q = q_ref[0, rows, pair * pw:(pair + 1) * pw]
                qzero = jnp.zeros_like(q)
                qm = (jnp.where(first, q, qzero), jnp.where(first, qzero, q))
                out = None
                for i in range(2):
                    h = 2 * pair + i
                    s = _dot_nt(qm[i], kk) + bias_ref[h]
                    s = jnp.where(kabs_ok, s, NEG)
                    sink = sink_ref[h] * LOG2E
                    m = jnp.maximum(jnp.max(s, axis=-1, keepdims=True), sink)
                    p = jnp.exp2(s - m)
                    denom = jnp.sum(p, axis=-1, keepdims=True) + jnp.exp2(sink - m)
                    o = _dot(p.astype(BF16), vm[i]) / denom
                    out = o if out is None else out + o
                o_ref[0, rows, pair * pw:(pair + 1) * pw] = out.astype(BF16)


def _swa(zq, kva, rel_bias, sinks):
    B, S, _ = zq.shape
    nb = S // (BLOCK * SWA_QBLOCKS)
    bucket = jnp.asarray(_t5_bucket_table())
    smem = pl.BlockSpec(memory_space=pltpu.SMEM)
    kv = lambda k: pl.BlockSpec((1, BLOCK, KVA_W), lambda b, n: (b, SWA_QBLOCKS * n + k, 0))
    return pl.pallas_call(
        _swa_kernel,
        grid=(B, nb),
        in_specs=[
            pl.BlockSpec((BLOCK, 2 * BLOCK), lambda b, n: (0, 0)),
            smem, smem,
            pl.BlockSpec((1, SWA_QBLOCKS * BLOCK, W_QA), lambda b, n: (b, n, 0)),
            *[kv(k) for k in range(SWA_QBLOCKS + 1)],
        ],
        out_specs=pl.BlockSpec((1, SWA_QBLOCKS * BLOCK, W_QA), lambda b, n: (b, n, 0)),
        out_shape=jax.ShapeDtypeStruct((B, S, W_QA), BF16),
        scratch_shapes=[pltpu.VMEM((N_HEADS_A, BLOCK, 2 * BLOCK), F32)],
        compiler_params=pltpu.CompilerParams(
            dimension_semantics=("arbitrary", "arbitrary"), vmem_limit_bytes=VMEM_LIMIT),
        name="swa",
    )(bucket, rel_bias, sinks, zq, *([kva] * (SWA_QBLOCKS + 1)))


FOX_TQ = 512
FOX_TK = 512
FOX_PAIRS = 4
C_MASKED = 1e30


def _fox_kernel(q_ref, k_ref, vt_ref, c_ref, o_ref):
    grp = pl.program_id(1)
    qi = pl.program_id(2)
    tq = q_ref.shape[1]
    tk = FOX_TK
    pw = 2 * HEAD_DIM
    q0 = BLOCK + qi * tq
    first = lax.broadcasted_iota(jnp.int32, (tq, pw), 1) < HEAD_DIM
    qm = []
    for pr in range(FOX_PAIRS):
        q = q_ref[0, :, pr * pw:(pr + 1) * pw]
        qzero = jnp.zeros_like(q)
        qm.append((jnp.where(first, q, qzero), jnp.where(first, qzero, q)))
    kpos = lax.broadcasted_iota(jnp.int32, (tk, tq), 0)
    qpos = q0 + lax.broadcasted_iota(jnp.int32, (tk, tq), 1)
    top = lax.broadcasted_iota(jnp.int32, (pw, tq), 0) < HEAD_DIM

    def ccols(start, size):
        c = c_ref[0, pl.ds(start, size), :]
        head = lax.broadcasted_iota(jnp.int32, c.shape, 1)
        return [[jnp.sum(jnp.where(head == 2 * (grp * FOX_PAIRS + pr) + i, c, 0.0), axis=1, keepdims=True)
                 for i in range(2)] for pr in range(FOX_PAIRS)]

    cq0 = [[col[0:1, :] for col in cols] for cols in ccols(pl.multiple_of(q0, BLOCK), 8)]

    def chunk(jc, carry, causal):
        ks = pl.multiple_of(jc * tk, tk)
        cc = ccols(ks, tk)
        out = []
        for pr in range(FOX_PAIRS):
            m, l, acc = carry[pr]
            kk = k_ref[0, pl.ds(ks, tk), pr * pw:(pr + 1) * pw]
            vt = vt_ref[0, pr * pw:(pr + 1) * pw, pl.ds(ks, tk)]
            vtop = lax.broadcasted_iota(jnp.int32, vt.shape, 0) < HEAD_DIM
            vzero = jnp.zeros_like(vt)
            vtm = (jnp.where(vtop, vt, vzero), jnp.where(vtop, vzero, vt))
            m_out, l_out, alphas, pv = [], [], [], None
            for i in range(2):
                s = _dot_nt(kk, qm[pr][i]) + (cq0[pr][i] - cc[pr][i]) * LOG2E
                if causal:
                    s = jnp.where(ks + kpos <= qpos, s, NEG)
                m_new = jnp.maximum(m[i], jnp.max(s, axis=0, keepdims=True))
                alpha = jnp.exp2(m[i] - m_new)
                p = jnp.exp2(s - m_new)
                m_out.append(m_new)
                l_out.append(alpha * l[i] + jnp.sum(p, axis=0, keepdims=True))
                alphas.append(alpha)
                d = _dot(vtm[i], p.astype(BF16))
                pv = d if pv is None else pv + d
            acc = jnp.where(top, alphas[0], alphas[1]) * acc + pv
            out.append((tuple(m_out), tuple(l_out), acc))
        return tuple(out)

    n_full = q0 // tk
    n_chunks = (q0 + tq + tk - 1) // tk
    neg = jnp.full((1, tq), NEG, F32)
    zero = jnp.zeros((1, tq), F32)
    carry = tuple(((neg, neg), (zero, zero), jnp.zeros((pw, tq), F32)) for _ in range(FOX_PAIRS))
    carry = lax.fori_loop(0, n_full, functools.partial(chunk, causal=False), carry)
    carry = lax.fori_loop(n_full, n_chunks, functools.partial(chunk, causal=True), carry)
    for pr in range(FOX_PAIRS):
        m, l, acc = carry[pr]
        o_ref[0, :, pr * pw:(pr + 1) * pw] = (acc / jnp.where(top, l[0], l[1])).T.astype(BF16)


def _fox(zq, kb, vbt, c):
    B, S, _ = zq.shape
    P = kb.shape[1]
    gw = FOX_PAIRS * 2 * HEAD_DIM
    ngrp = W_B // gw
    qblk0 = W_QA // gw
    return pl.pallas_call(
        _fox_kernel,
        grid=(B, ngrp, S // FOX_TQ),
        in_specs=[
            pl.BlockSpec((1, FOX_TQ, gw), lambda b, p, i: (b, i, qblk0 + p)),
            pl.BlockSpec((1, P, gw), lambda b, p, i: (b, 0, p)),
            pl.BlockSpec((1, gw, P), lambda b, p, i: (b, p, 0)),
            pl.BlockSpec((1, P, LANES), lambda b, p, i: (b, 0, 0)),
        ],
        out_specs=pl.BlockSpec((1, FOX_TQ, gw), lambda b, p, i: (b, i, p)),
        out_shape=jax.ShapeDtypeStruct((B, S, W_B), BF16),
        compiler_params=pltpu.CompilerParams(
            dimension_semantics=("arbitrary", "arbitrary", "arbitrary"), vmem_limit_bytes=VMEM_LIMIT),
        name="fox",
    )(zq, kb, vbt, c)


MERGE_TM = 512


def _merge_kernel(x_ref, oa_ref, ob_ref, ga_ref, gb_ref, g0_ref, b0_ref, wpa_ref, wpb_ref, wo_ref,
                  g1_ref, b1_ref, h1_ref, h1t_ref):
    h = _ln(x_ref[...], g0_ref[...], b0_ref[...])
    ya = _dot(oa_ref[...], wpa_ref[...])
    yb = _dot(ob_ref[...], wpb_ref[...])
    merged = jax.nn.sigmoid(ga_ref[...].astype(F32)) * ya + jax.nn.sigmoid(gb_ref[...].astype(F32)) * yb
    y = _dot(merged.astype(BF16), wo_ref[...])
    h1 = _ln(DEEPNORM_ALPHA * h + y, g1_ref[...], b1_ref[...])
    h1_ref[...] = h1
    h1t_ref[...] = h1.T.astype(BF16)


def _merge(x2, oa2, ob2, zq2, g0, b0, wpa, wpb, wo, g1, b1):
    T, D = x2.shape
    tm = MERGE_TM
    full = lambda a: pl.BlockSpec(a.shape, lambda i: (0,) * a.ndim)
    gate0 = (W_QA + W_B) // D
    return pl.pallas_call(
        _merge_kernel,
        grid=(T // tm,),
        in_specs=[
            pl.BlockSpec((tm, D), lambda i: (i, 0)),
            pl.BlockSpec((tm, W_QA), lambda i: (i, 0)),
            pl.BlockSpec((tm, W_B), lambda i: (i, 0)),
            pl.BlockSpec((tm, D), lambda i: (i, gate0)),
            pl.BlockSpec((tm, D), lambda i: (i, gate0 + 1)),
            full(g0), full(b0), full(wpa), full(wpb), full(wo), full(g1), full(b1),
        ],
        out_specs=[pl.BlockSpec((tm, D), lambda i: (i, 0)), pl.BlockSpec((D, tm), lambda i: (0, i))],
        out_shape=[jax.ShapeDtypeStruct((T, D), F32), jax.ShapeDtypeStruct((D, T), BF16)],
        compiler_params=pltpu.CompilerParams(
            dimension_semantics=("arbitrary",), vmem_limit_bytes=VMEM_LIMIT),
        name="merge",
    )(x2, oa2, ob2, zq2, zq2, g0, b0, wpa, wpb, wo, g1, b1)


ROUTE_TM = 512
ROUTE_UNROLL = 2
LANES = 128
HALF_DQ = PEER_DQ // 2
RANK_OUT = 64.0


def _sort16_network():
    pairs = []

    def merge(lo, n, r):
        step = 2 * r
        if step < n:
            merge(lo, n, step)
            merge(lo + r, n, step)
            pairs.extend((i, i + r) for i in range(lo + r, lo + n - r, step))
        else:
            pairs.append((lo, lo + r))

    def sort(lo, n):
        if n > 1:
            sort(lo, n // 2)
            sort(lo + n // 2, n // 2)
            merge(lo, n, 1)

    sort(0, 16)
    return tuple(pairs)


def _top_values(s):
    rows = s.shape[0] // 16
    v = [s[i * rows:(i + 1) * rows] for i in range(16)]
    for i, j in _sort16_network():
        v[i], v[j] = jnp.maximum(v[i], v[j]), jnp.minimum(v[i], v[j])
    tops = []
    for r in range(PEER_TOPK):
        m = jnp.max(v[0], axis=0, keepdims=True)
        tops.append(m)
        if r + 1 < PEER_TOPK:
            drawn = v[0] == m
            for k in range(PEER_TOPK - 1 - r):
                v[k] = jnp.where(drawn, v[k + 1], v[k])
    return tops


def _stack_rows(rows):
    n = len(rows)
    ri = lax.broadcasted_iota(jnp.int32, (n, rows[0].shape[1]), 0)
    out = jnp.broadcast_to(rows[0], ri.shape)
    for r in range(1, n):
        out = jnp.where(ri == r, rows[r], out)
    return out


def _route_kernel(ht_ref, wqt_ref, sk_ref, na_ref, e1_ref, rb_ref, e2_ref, q_ref, sc_ref):
    K = PEER_TOPK
    q_ref[...] = _dot(wqt_ref[...], ht_ref[...]).astype(BF16)
    for h in range(PEER_HEADS):
        sc_ref[h] = _dot(sk_ref[h], q_ref[h * PEER_DQ:(h + 1) * PEER_DQ, :])

    n_groups = ht_ref.shape[1] // LANES

    def one_group(h, g):
        lanes = pl.ds(pl.multiple_of(g * LANES, LANES), LANES)
        s1 = sc_ref[h, :PEER_NKEYS, lanes]
        s2 = sc_ref[h, PEER_NKEYS:, lanes]
        a = _top_values(s1)
        b = _top_values(s2)
        rb = jnp.full(s2.shape, RANK_OUT, F32)
        for r in range(K):
            rb = jnp.where(s2 == b[r], float(r), rb)
        bmat16 = _stack_rows(b)
        bmat8 = bmat16[:8]
        cands = [a[0] + bmat16]
        for i in range(1, 8):
            cands.append(a[i] + bmat8)
        cands.append(_stack_rows(a[8:]) + b[0])
        cand = jnp.concatenate(cands, axis=0)
        vmax = a[0] + b[0]
        z = jnp.zeros_like(vmax)
        for r in range(K):
            m = jnp.max(cand, axis=0, keepdims=True)
            cand = jnp.where(cand == m, -jnp.inf, cand)
            z = z + jnp.exp(m - vmax)
        sel = jnp.where(cand == -jnp.inf, 1.0, 0.0)
        counts = [jnp.sum(sel[0:16], axis=0, keepdims=True)]
        for i in range(1, 8):
            counts.append(jnp.sum(sel[8 + 8 * i:16 + 8 * i], axis=0, keepdims=True))
        n_tail = jnp.sum(sel[72:80], axis=0, keepdims=True)
        low = jnp.full_like(n_tail, jnp.inf)
        for k in range(1, 9):
            low = jnp.where(n_tail == float(k), a[7 + k], low)
        na = jnp.where(s1 >= low, jnp.where(s1 < a[7], 1.0, 0.0), 0.0)
        for i in range(8):
            na = jnp.where(s1 == a[i], counts[i], na)
        na_ref[h, :, lanes] = na
        e1_ref[h, :, lanes] = jnp.exp(s1 - a[0])
        rb_ref[h, :, lanes] = rb.astype(BF16)
        e2_ref[h, :, lanes] = (jnp.exp(s2 - b[0]) * (0.5 / z)).astype(BF16)

    per_head = n_groups // ROUTE_UNROLL

    def group(idx, _):
        h = idx // per_head
        g0 = (idx % per_head) * ROUTE_UNROLL
        for k in range(ROUTE_UNROLL):
            one_group(h, g0 + k)
        return 0

    lax.fori_loop(0, PEER_HEADS * per_head, group, 0)


def _route(h1t, wqt, skbd):
    D, T = h1t.shape
    tm = ROUTE_TM
    full = lambda a: pl.BlockSpec(a.shape, lambda i: (0,) * a.ndim)
    ospec = pl.BlockSpec((PEER_HEADS, PEER_NKEYS, tm), lambda i: (0, 0, i))
    oshape = lambda dt: jax.ShapeDtypeStruct((PEER_HEADS, PEER_NKEYS, T), dt)
    return pl.pallas_call(
        _route_kernel,
        grid=(T // tm,),
        in_specs=[pl.BlockSpec((D, tm), lambda i: (0, i)), full(wqt), full(skbd)],
        out_specs=[ospec, ospec, ospec, ospec],
        out_shape=[oshape(F32), oshape(F32), oshape(BF16), oshape(BF16)],
        scratch_shapes=[pltpu.VMEM((PEER_HEADS * PEER_DQ, tm), BF16),
                        pltpu.VMEM((PEER_HEADS, 2 * PEER_NKEYS, tm), F32)],
        compiler_params=pltpu.CompilerParams(
            dimension_semantics=("arbitrary",), vmem_limit_bytes=VMEM_LIMIT),
        name="route",
    )(h1t, wqt, skbd)


EXP_TM = 512
EXP_TE = 2048
EXP_IA = EXP_TE // PEER_NKEYS
BF16_ROWS = 16
SQRT_HALF = 0.7071067811865476


def _expert_kernel(xt_ref, h1_ref, u_ref, v_ref, na_ref, e1_ref, rb_ref, e2_ref, g2_ref, b2_ref,
                   o_ref, acc_ref, hu_ref, a_ref):
    j = pl.program_id(1)
    nj = pl.num_programs(1)
    tm = xt_ref.shape[1]
    nsub = PEER_NKEYS // BF16_ROWS

    @pl.when(j == 0)
    def _():
        acc_ref[...] = jnp.zeros_like(acc_ref)

    hu_ref[...] = _dot(u_ref[...], xt_ref[...])
    rows = pl.ds(pl.multiple_of(j * EXP_IA, EXP_IA), EXP_IA)
    for il in range(EXP_IA):
        sl = slice(il * PEER_NKEYS, (il + 1) * PEER_NKEYS)
        hu = hu_ref[sl, :]
        hb = hu.astype(BF16)
        act = hb * (1.0 + lax.erf(hb * SQRT_HALF))
        gate = jnp.zeros((nsub, BF16_ROWS, tm), BF16)
        for h in range(PEER_HEADS):
            na = jnp.broadcast_to(na_ref[h, rows, :][il:il + 1], (BF16_ROWS, tm)).astype(BF16)
            e1 = jnp.broadcast_to(e1_ref[h, rows, :][il:il + 1], (BF16_ROWS, tm)).astype(BF16)
            rb = rb_ref[h].reshape(nsub, BF16_ROWS, tm)
            w = e2_ref[h].reshape(nsub, BF16_ROWS, tm) * e1[None]
            gate = gate + jnp.where(rb < na[None], w, jnp.zeros_like(w))
        a_ref[sl, :] = act * gate.reshape(PEER_NKEYS, tm)
    acc_ref[...] += _dot_tn(a_ref[...], v_ref[...])

    @pl.when(j == nj - 1)
    def _():
        o_ref[...] = _ln(DEEPNORM_ALPHA * h1_ref[...] + acc_ref[...], g2_ref[...], b2_ref[...])


def _experts(h1t, h1, u, v, na, e1, rb, e2, g2, b2):
    T, D = h1.shape
    E = u.shape[0]
    tm, te = EXP_TM, EXP_TE
    gspec = pl.BlockSpec((PEER_HEADS, PEER_NKEYS, tm), lambda i, j: (0, 0, i))
    vec = pl.BlockSpec((1, D), lambda i, j: (0, 0))
    return pl.pallas_call(
        _expert_kernel,
        grid=(T // tm, E // te),
        in_specs=[
            pl.BlockSpec((D, tm), lambda i, j: (0, i)),
            pl.BlockSpec((tm, D), lambda i, j: (i, 0)),
            pl.BlockSpec((te, D), lambda i, j: (j, 0)),
            pl.BlockSpec((te, D), lambda i, j: (j, 0)),
            gspec, gspec, gspec, gspec, vec, vec,
        ],
        out_specs=pl.BlockSpec((tm, D), lambda i, j: (i, 0)),
        out_shape=jax.ShapeDtypeStruct((T, D), F32),
        scratch_shapes=[pltpu.VMEM((tm, D), F32), pltpu.VMEM((te, tm), F32), pltpu.VMEM((te, tm), BF16)],
        compiler_params=pltpu.CompilerParams(
            dimension_semantics=("arbitrary", "arbitrary"), vmem_limit_bytes=VMEM_LIMIT),
        name="experts",
    )(h1t, h1, u, v, na, e1, rb, e2, g2, b2)


def kernel(x, meta_tokens, emb_ln_g, emb_ln_b, rel_bias, w_in, b_in, attn_sinks, w_pa, w_pb, w_o,
           ln1_g, ln1_b, peer_wq, peer_subkeys, peer_u, peer_v, ln2_g, ln2_b):
    B, S, D = x.shape
    T = B * S
    row = lambda a: a.reshape(1, -1).astype(F32)

    w, bi = w_in[0], b_in[0]
    sizes = (W_QA, W_KA, W_KA, W_B, W_B, W_B, N_HEADS_B, D, D)
    offs = np.cumsum((0,) + sizes)
    col = lambda a, k: a[..., offs[k]:offs[k + 1]]
    scale = HEAD_DIM ** -0.5 * LOG2E
    wq = jnp.concatenate([col(w, 0) * scale, col(w, 3) * scale, col(w, 7), col(w, 8)], axis=-1).astype(BF16)
    bq = jnp.concatenate([col(bi, 0) * scale, col(bi, 3) * scale, col(bi, 7), col(bi, 8)], axis=-1)
    dup = lambda a: jnp.repeat(a.reshape(a.shape[:-1] + (N_KV_A, 1, HEAD_DIM)), 2, axis=-2).reshape(
        a.shape[:-1] + (2 * W_KA,))
    wkv = jnp.concatenate([dup(col(w, 1)), dup(col(w, 2)), col(w, 4)], axis=-1).astype(BF16)
    bkv = jnp.concatenate([dup(col(bi, 1)), dup(col(bi, 2)), col(bi, 4)], axis=-1)
    wvbt = col(w, 5).T.astype(BF16)
    bvbt = col(bi, 5).reshape(-1, 1).astype(F32)
    fpad = LANES - N_HEADS_B
    wfb = jnp.pad(col(w, 6), ((0, 0), (0, fpad))).astype(BF16)
    bfb = jnp.pad(col(bi, 6), (0, fpad))

    g0, b0 = row(emb_ln_g), row(emb_ln_b)
    kva, kb, vbt, c, zq = _inproj(x, meta_tokens.astype(F32), g0, b0, wkv, row(bkv), wvbt, bvbt,
                                  wfb, row(bfb), wq, row(bq))
    oa = _swa(zq, kva, rel_bias.astype(F32), attn_sinks[0].astype(F32))
    ob = _fox(zq, kb, vbt, c)

    h1, h1t = _merge(x.reshape(T, D), oa.reshape(T, W_QA), ob.reshape(T, W_B), zq.reshape(T, -1),
                     g0, b0, w_pa[0].astype(BF16), w_pb[0].astype(BF16), w_o[0].astype(BF16),
                     row(ln1_g[0]), row(ln1_b[0]))

    wqt = peer_wq[0].T.astype(BF16)
    sk = peer_subkeys[0].astype(BF16)
    zero = jnp.zeros_like(sk[:, 0])
    skbd = jnp.concatenate([jnp.concatenate([sk[:, 0], zero], axis=-1),
                            jnp.concatenate([zero, sk[:, 1]], axis=-1)], axis=1)
    na, e1, rb, e2 = _route(h1t, wqt, skbd)
    out = _experts(h1t, h1, peer_u[0].astype(BF16), peer_v[0].astype(BF16), na, e1, rb, e2,
                   row(ln2_g[0]), row(ln2_b[0]))
    return out.reshape(B, S, D)
```

```python
import functools
import math

import jax
import jax.numpy as jnp
import numpy as np
from jax import lax
from jax.experimental import pallas as pl
from jax.experimental.pallas import tpu as pltpu

N_META = 16
BLOCK = 128
PAD = BLOCK - N_META
HEAD_DIM = 64
N_HEADS_A = 8
N_KV_A = 2
GROUP_A = N_HEADS_A // N_KV_A
WINDOW = 128
N_HEADS_B = 8
T5_BUCKETS = 32
T5_MAX_DIST = 128
PEER_HEADS = 8
PEER_NKEYS = 128
PEER_DQ = 256
PEER_TOPK = 16
LN_EPS = 1e-5
NEG = -1e30
LOG2E = 1.4426950408889634
DEPTH = 1
DEEPNORM_ALPHA = (2 * DEPTH) ** 0.25

W_QA = N_HEADS_A * HEAD_DIM
W_KA = N_KV_A * HEAD_DIM
W_B = N_HEADS_B * HEAD_DIM
KVA_W = 4 * W_KA

F32 = jnp.float32
BF16 = jnp.bfloat16

VMEM_LIMIT = 56 * 1024 * 1024
VMEM_LIMIT_INPROJ = 60 * 1024 * 1024


def _ln(x, g, b):
    mu = jnp.mean(x, axis=-1, keepdims=True)
    xc = x - mu
    var = jnp.mean(xc * xc, axis=-1, keepdims=True)
    return xc * lax.rsqrt(var + LN_EPS) * g + b


def _dot(a, b):
    return jnp.dot(a, b, preferred_element_type=F32)


def _dot_nt(a, b):
    return lax.dot_general(a, b, (((1,), (1,)), ((), ())), preferred_element_type=F32)


def _dot_tn(a, b):
    return lax.dot_general(a, b, (((0,), (0,)), ((), ())), preferred_element_type=F32)


def _split3(x):
    hi = x.astype(BF16)
    r = x - hi.astype(F32)
    mid = r.astype(BF16)
    lo = (r - mid.astype(F32)).astype(BF16)
    return hi, mid, lo


QT_COLS = 512


def _inproj_kernel(x_ref, meta_ref, g0_ref, b0_ref, wkv_ref, bkv_ref, wvbt_ref, bvbt_ref,
                   wfb_ref, bfb_ref, wq_ref, bq_ref,
                   kva_ref, kb_ref, vbt_ref, c_ref, zq_ref, hs_ref):
    j = pl.program_id(1)
    P = hs_ref.shape[0]

    @pl.when(j == 0)
    def _():
        g0 = g0_ref[...]
        b0 = b0_ref[...]
        hs_ref[0:PAD, :] = jnp.zeros((PAD, hs_ref.shape[1]), BF16)
        hs_ref[PAD:BLOCK, :] = _ln(meta_ref[...], g0, b0).astype(BF16)
        hs_ref[BLOCK:, :] = _ln(x_ref[0], g0, b0).astype(BF16)
        hs = hs_ref[...]
        nkv = KVA_W
        kva_ref[0] = (_dot(hs, wkv_ref[:, :nkv]) + bkv_ref[:, :nkv]).astype(BF16)
        kb_ref[0, :P, :] = (_dot(hs, wkv_ref[:, nkv:]) + bkv_ref[:, nkv:]).astype(BF16)
        kb_ref[0, P:, :] = jnp.zeros((kb_ref.shape[1] - P, kb_ref.shape[2]), BF16)
        vbt_ref[0, :, :P] = (_dot_nt(wvbt_ref[...], hs) + bvbt_ref[...]).astype(BF16)
        vbt_ref[0, :, P:] = jnp.zeros((vbt_ref.shape[1], vbt_ref.shape[2] - P), BF16)
        c_ref[0, P:, :] = jnp.full((c_ref.shape[1] - P, c_ref.shape[2]), C_MASKED, F32)
        lf = jax.nn.log_sigmoid(_dot(hs, wfb_ref[...]) + bfb_ref[...])
        pos = lax.broadcasted_iota(jnp.int32, lf.shape, 0)
        lf = jnp.where(pos >= PAD, lf, 0.0)
        ri = lax.broadcasted_iota(jnp.int32, (BLOCK, BLOCK), 0)
        ci = lax.broadcasted_iota(jnp.int32, (BLOCK, BLOCK), 1)
        tri = jnp.where(ci <= ri, 1.0, 0.0).astype(BF16)
        carry = jnp.zeros((1, lf.shape[1]), F32)
        for n in range(P // BLOCK):
            hi, mid, lo = _split3(lf[n * BLOCK:(n + 1) * BLOCK, :])
            cb = _dot(tri, hi) + _dot(tri, mid) + _dot(tri, lo) + carry
            carry = cb[BLOCK - 1:BLOCK, :]
            if n == 0:
                cb = jnp.where(lax.broadcasted_iota(jnp.int32, cb.shape, 0) >= PAD, cb, C_MASKED)
            c_ref[0, n * BLOCK:(n + 1) * BLOCK, :] = cb

    @pl.when(j > 0)
    def _():
        zq_ref[0] = (_dot(hs_ref[BLOCK:, :], wq_ref[...]) + bq_ref[...]).astype(BF16)


def _inproj(x, meta, g0, b0, wkv, bkv, wvbt, bvbt, wfb, bfb, wq, bq):
    B, S, D = x.shape
    P = S + BLOCK
    P2 = -(-P // FOX_TK) * FOX_TK
    nq = wq.shape[1] // QT_COLS
    full = lambda shape: pl.BlockSpec(shape, lambda b, j: (0,) * len(shape))
    qcol = lambda b, j: (0, jnp.maximum(j - 1, 0))
    return pl.pallas_call(
        _inproj_kernel,
        grid=(B, nq + 1),
        in_specs=[
            pl.BlockSpec((1, S, D), lambda b, j: (b, 0, 0)),
            full(meta.shape), full(g0.shape), full(b0.shape),
            full(wkv.shape), full(bkv.shape), full(wvbt.shape), full(bvbt.shape),
            full(wfb.shape), full(bfb.shape),
            pl.BlockSpec((D, QT_COLS), qcol),
            pl.BlockSpec((1, QT_COLS), qcol),
        ],
        out_specs=[
            pl.BlockSpec((1, P, KVA_W), lambda b, j: (b, 0, 0)),
            pl.BlockSpec((1, P2, W_B), lambda b, j: (b, 0, 0)),
            pl.BlockSpec((1, W_B, P2), lambda b, j: (b, 0, 0)),
            pl.BlockSpec((1, P2, LANES), lambda b, j: (b, 0, 0)),
            pl.BlockSpec((1, S, QT_COLS), lambda b, j: (b, 0, jnp.maximum(j - 1, 0))),
        ],
        out_shape=[
            jax.ShapeDtypeStruct((B, P, KVA_W), BF16),
            jax.ShapeDtypeStruct((B, P2, W_B), BF16),
            jax.ShapeDtypeStruct((B, W_B, P2), BF16),
            jax.ShapeDtypeStruct((B, P2, LANES), F32),
            jax.ShapeDtypeStruct((B, S, wq.shape[1]), BF16),
        ],
        scratch_shapes=[pltpu.VMEM((P, D), BF16)],
        compiler_params=pltpu.CompilerParams(
            dimension_semantics=("arbitrary", "arbitrary"), vmem_limit_bytes=VMEM_LIMIT_INPROJ),
        name="inproj",
    )(x, meta, g0, b0, wkv, bkv, wvbt, bvbt, wfb, bfb, wq, bq)


def _t5_bucket_table():
    qi = np.arange(BLOCK)[:, None] + BLOCK
    kj = np.arange(2 * BLOCK)[None, :]
    n = np.maximum(qi - kj, 0)
    max_exact = T5_BUCKETS // 2
    nf = np.maximum(n, 1).astype(np.float32)
    large = max_exact + (np.log(nf / np.float32(max_exact)) / np.float32(math.log(T5_MAX_DIST / max_exact))
                         * (T5_BUCKETS - max_exact)).astype(np.int32)
    large = np.minimum(large, T5_BUCKETS - 1)
    bucket = np.where(n < max_exact, n, large).astype(np.int32)
    dist = qi - kj
    window = (dist >= 0) & (dist < WINDOW)
    return np.where(window, bucket, -1).astype(np.int32)


SWA_QBLOCKS = 8


def _swa_kernel(bucket_ref, rel_ref, sink_ref, q_ref, *rest):
    kv_refs, o_ref, bias_ref = rest[:SWA_QBLOCKS + 1], rest[SWA_QBLOCKS + 1], rest[SWA_QBLOCKS + 2]
    b = pl.program_id(0)
    n = pl.program_id(1)

    @pl.when((b == 0) & (n == 0))
    def _():
        bucket = bucket_ref[...]
        for h in range(N_HEADS_A):
            acc = jnp.where(bucket < 0, NEG, 0.0).astype(F32)
            for t in range(T5_BUCKETS):
                acc = jnp.where(bucket == t, rel_ref[t, h] * LOG2E, acc)
            bias_ref[h] = acc

    kcol = lax.broadcasted_iota(jnp.int32, (BLOCK, 2 * BLOCK), 1)
    pw = 2 * HEAD_DIM
    first = lax.broadcasted_iota(jnp.int32, (BLOCK, pw), 1) < HEAD_DIM
    vfirst = lax.broadcasted_iota(jnp.int32, (2 * BLOCK, pw), 1) < HEAD_DIM
    for sub in range(SWA_QBLOCKS):
        kvp_ref, kvc_ref = kv_refs[sub], kv_refs[sub + 1]
        rows = slice(sub * BLOCK, (sub + 1) * BLOCK)
        kabs_ok = ((n * SWA_QBLOCKS + sub) * BLOCK + kcol) >= PAD
        for g in range(N_KV_A):
            kk = jnp.concatenate([kvp_ref[0, :, g * pw:(g + 1) * pw],
                                  kvc_ref[0, :, g * pw:(g + 1) * pw]], axis=0)
            vv = jnp.concatenate([kvp_ref[0, :, (N_KV_A + g) * pw:(N_KV_A + g + 1) * pw],
                                  kvc_ref[0, :, (N_KV_A + g) * pw:(N_KV_A + g + 1) * pw]], axis=0)
            vzero = jnp.zeros_like(vv)
            vm = (jnp.where(vfirst, vv, vzero), jnp.where(vfirst, vzero, vv))
            for pp in range(GROUP_A // 2):
                pair = g * (GROUP_A // 2) + pp
                q = q_ref[0, rows, pair * pw:(pair + 1) * pw]
                qzero = jnp.zeros_like(q)
                qm = (jnp.where(first, q, qzero), jnp.where(first, qzero, q))
                out = None
                for i in range(2):
                    h = 2 * pair + i
                    s = _dot_nt(qm[i], kk) + bias_ref[h]
                    s = jnp.where(kabs_ok, s, NEG)
                    sink = sink_ref[h] * LOG2E
                    m = jnp.maximum(jnp.max(s, axis=-1, keepdims=True), sink)
                    p = jnp.exp2(s - m)
                    denom = jnp.sum(p, axis=-1, keepdims=True) + jnp.exp2(sink - m)
                    o = _dot(p.astype(BF16), vm[i]) / denom
                    out = o if out is None else out + o
                o_ref[0, rows, pair * pw:(pair + 1) * pw] = out.astype(BF16)


def _swa(zq, kva, rel_bias, sinks):
    B, S, _ = zq.shape
    nb = S // (BLOCK * SWA_QBLOCKS)
    bucket = jnp.asarray(_t5_bucket_table())
    smem = pl.BlockSpec(memory_space=pltpu.SMEM)
    kv = lambda k: pl.BlockSpec((1, BLOCK, KVA_W), lambda b, n: (b, SWA_QBLOCKS * n + k, 0))
    return pl.pallas_call(
        _swa_kernel,
        grid=(B, nb),
        in_specs=[
            pl.BlockSpec((BLOCK, 2 * BLOCK), lambda b, n: (0, 0)),
            smem, smem,
            pl.BlockSpec((1, SWA_QBLOCKS * BLOCK, W_QA), lambda b, n: (b, n, 0)),
            *[kv(k) for k in range(SWA_QBLOCKS + 1)],
        ],
        out_specs=pl.BlockSpec((1, SWA_QBLOCKS * BLOCK, W_QA), lambda b, n: (b, n, 0)),
        out_shape=jax.ShapeDtypeStruct((B, S, W_QA), BF16),
        scratch_shapes=[pltpu.VMEM((N_HEADS_A, BLOCK, 2 * BLOCK), F32)],
        compiler_params=pltpu.CompilerParams(
            dimension_semantics=("arbitrary", "arbitrary"), vmem_limit_bytes=VMEM_LIMIT),
        name="swa",
    )(bucket, rel_bias, sinks, zq, *([kva] * (SWA_QBLOCKS + 1)))


FOX_TQ = 512
FOX_TK = 512
FOX_PAIRS = 4
C_MASKED = 1e30


def _fox_kernel(q_ref, k_ref, vt_ref, c_ref, o_ref):
    grp = pl.program_id(1)
    qi = pl.program_id(2)
    tq = q_ref.shape[1]
    tk = FOX_TK
    pw = 2 * HEAD_DIM
    q0 = BLOCK + qi * tq
    first = lax.broadcasted_iota(jnp.int32, (tq, pw), 1) < HEAD_DIM
    qm = []
    for pr in range(FOX_PAIRS):
        q = q_ref[0, :, pr * pw:(pr + 1) * pw]
        qzero = jnp.zeros_like(q)
        qm.append((jnp.where(first, q, qzero), jnp.where(first, qzero, q)))
    kpos = lax.broadcasted_iota(jnp.int32, (tk, tq), 0)
    qpos = q0 + lax.broadcasted_iota(jnp.int32, (tk, tq), 1)
    top = lax.broadcasted_iota(jnp.int32, (pw, tq), 0) < HEAD_DIM

    def ccols(start, size):
        c = c_ref[0, pl.ds(start, size), :]
        head = lax.broadcasted_iota(jnp.int32, c.shape, 1)
        return [[jnp.sum(jnp.where(head == 2 * (grp * FOX_PAIRS + pr) + i, c, 0.0), axis=1, keepdims=True)
                 for i in range(2)] for pr in range(FOX_PAIRS)]

    cq0 = [[col[0:1, :] for col in cols] for cols in ccols(pl.multiple_of(q0, BLOCK), 8)]

    def chunk(jc, carry, causal):
        ks = pl.multiple_of(jc * tk, tk)
        cc = ccols(ks, tk)
        out = []
        for pr in range(FOX_PAIRS):
            m, l, acc = carry[pr]
            kk = k_ref[0, pl.ds(ks, tk), pr * pw:(pr + 1) * pw]
            vt = vt_ref[0, pr * pw:(pr + 1) * pw, pl.ds(ks, tk)]
            vtop = lax.broadcasted_iota(jnp.int32, vt.shape, 0) < HEAD_DIM
            vzero = jnp.zeros_like(vt)
            vtm = (jnp.where(vtop, vt, vzero), jnp.where(vtop, vzero, vt))
            m_out, l_out, alphas, pv = [], [], [], None
            for i in range(2):
                s = _dot_nt(kk, qm[pr][i]) + (cq0[pr][i] - cc[pr][i]) * LOG2E
                if causal:
                    s = jnp.where(ks + kpos <= qpos, s, NEG)
                m_new = jnp.maximum(m[i], jnp.max(s, axis=0, keepdims=True))
                alpha = jnp.exp2(m[i] - m_new)
                p = jnp.exp2(s - m_new)
                m_out.append(m_new)
                l_out.append(alpha * l[i] + jnp.sum(p, axis=0, keepdims=True))
                alphas.append(alpha)
                d = _dot(vtm[i], p.astype(BF16))
                pv = d if pv is None else pv + d
            acc = jnp.where(top, alphas[0], alphas[1]) * acc + pv
            out.append((tuple(m_out), tuple(l_out), acc))
        return tuple(out)

    n_full = q0 // tk
    n_chunks = (q0 + tq + tk - 1) // tk
    neg = jnp.full((1, tq), NEG, F32)
    zero = jnp.zeros((1, tq), F32)
    carry = tuple(((neg, neg), (zero, zero), jnp.zeros((pw, tq), F32)) for _ in range(FOX_PAIRS))
    carry = lax.fori_loop(0, n_full, functools.partial(chunk, causal=False), carry)
    carry = lax.fori_loop(n_full, n_chunks, functools.partial(chunk, causal=True), carry)
    for pr in range(FOX_PAIRS):
        m, l, acc = carry[pr]
        o_ref[0, :, pr * pw:(pr + 1) * pw] = (acc / jnp.where(top, l[0], l[1])).T.astype(BF16)


def _fox(zq, kb, vbt, c):
    B, S, _ = zq.shape
    P = kb.shape[1]
    gw = FOX_PAIRS * 2 * HEAD_DIM
    ngrp = W_B // gw
    qblk0 = W_QA // gw
    return pl.pallas_call(
        _fox_kernel,
        grid=(B, ngrp, S // FOX_TQ),
        in_specs=[
            pl.BlockSpec((1, FOX_TQ, gw), lambda b, p, i: (b, i, qblk0 + p)),
            pl.BlockSpec((1, P, gw), lambda b, p, i: (b, 0, p)),
            pl.BlockSpec((1, gw, P), lambda b, p, i: (b, p, 0)),
            pl.BlockSpec((1, P, LANES), lambda b, p, i: (b, 0, 0)),
        ],
        out_specs=pl.BlockSpec((1, FOX_TQ, gw), lambda b, p, i: (b, i, p)),
        out_shape=jax.ShapeDtypeStruct((B, S, W_B), BF16),
        compiler_params=pltpu.CompilerParams(
            dimension_semantics=("arbitrary", "arbitrary", "arbitrary"), vmem_limit_bytes=VMEM_LIMIT),
        name="fox",
    )(zq, kb, vbt, c)


MERGE_TM = 512


def _merge_kernel(x_ref, oa_ref, ob_ref, ga_ref, gb_ref, g0_ref, b0_ref, wpa_ref, wpb_ref, wo_ref,
                  g1_ref, b1_ref, h1_ref, h1t_ref):
    h = _ln(x_ref[...], g0_ref[...], b0_ref[...])
    ya = _dot(oa_ref[...], wpa_ref[...])
    yb = _dot(ob_ref[...], wpb_ref[...])
    merged = jax.nn.sigmoid(ga_ref[...].astype(F32)) * ya + jax.nn.sigmoid(gb_ref[...].astype(F32)) * yb
    y = _dot(merged.astype(BF16), wo_ref[...])
    h1 = _ln(DEEPNORM_ALPHA * h + y, g1_ref[...], b1_ref[...])
    h1_ref[...] = h1
    h1t_ref[...] = h1.T.astype(BF16)


def _merge(x2, oa2, ob2, zq2, g0, b0, wpa, wpb, wo, g1, b1):
    T, D = x2.shape
    tm = MERGE_TM
    full = lambda a: pl.BlockSpec(a.shape, lambda i: (0,) * a.ndim)
    gate0 = (W_QA + W_B) // D
    return pl.pallas_call(
        _merge_kernel,
        grid=(T // tm,),
        in_specs=[
            pl.BlockSpec((tm, D), lambda i: (i, 0)),
            pl.BlockSpec((tm, W_QA), lambda i: (i, 0)),
            pl.BlockSpec((tm, W_B), lambda i: (i, 0)),
            pl.BlockSpec((tm, D), lambda i: (i, gate0)),
            pl.BlockSpec((tm, D), lambda i: (i, gate0 + 1)),
            full(g0), full(b0), full(wpa), full(wpb), full(wo), full(g1), full(b1),
        ],
        out_specs=[pl.BlockSpec((tm, D), lambda i: (i, 0)), pl.BlockSpec((D, tm), lambda i: (0, i))],
        out_shape=[jax.ShapeDtypeStruct((T, D), F32), jax.ShapeDtypeStruct((D, T), BF16)],
        compiler_params=pltpu.CompilerParams(
            dimension_semantics=("arbitrary",), vmem_limit_bytes=VMEM_LIMIT),
        name="merge",
    )(x2, oa2, ob2, zq2, zq2, g0, b0, wpa, wpb, wo, g1, b1)


ROUTE_TM = 512
ROUTE_UNROLL = 2
LANES = 128
HALF_DQ = PEER_DQ // 2


def _sort16_network():
    pairs = []

    def merge(lo, n, r):
        step = 2 * r
        if step < n:
            merge(lo, n, step)
            merge(lo + r, n, step)
            pairs.extend((i, i + r) for i in range(lo + r, lo + n - r, step))
        else:
            pairs.append((lo, lo + r))

    def sort(lo, n):
        if n > 1:
            sort(lo, n // 2)
            sort(lo + n // 2, n // 2)
            merge(lo, n, 1)

    sort(0, 16)
    return tuple(pairs)


def _top_values(s):
    rows = s.shape[0] // 16
    v = [s[i * rows:(i + 1) * rows] for i in range(16)]
    for i, j in _sort16_network():
        v[i], v[j] = jnp.maximum(v[i], v[j]), jnp.minimum(v[i], v[j])
    tops = []
    for r in range(PEER_TOPK):
        m = jnp.max(v[0], axis=0, keepdims=True)
        tops.append(m)
        if r + 1 < PEER_TOPK:
            drawn = v[0] == m
            for k in range(PEER_TOPK - 1 - r):
                v[k] = jnp.where(drawn, v[k + 1], v[k])
    return tops


def _count_greater(tops, x):
    c8 = tops[7] > x
    c4 = jnp.where(c8, tops[11], tops[3]) > x
    t2 = jnp.where(c8, jnp.where(c4, tops[13], tops[9]), jnp.where(c4, tops[5], tops[1]))
    c2 = t2 > x
    hi = jnp.where(c4, jnp.where(c2, tops[14], tops[12]), jnp.where(c2, tops[10], tops[8]))
    lo = jnp.where(c4, jnp.where(c2, tops[6], tops[4]), jnp.where(c2, tops[2], tops[0]))
    c1 = jnp.where(c8, hi, lo) > x
    count = (jnp.where(c8, 8.0, 0.0) + jnp.where(c4, 4.0, 0.0)) + (jnp.where(c2, 2.0, 0.0) + jnp.where(c1, 1.0, 0.0))
    return jnp.where(tops[15] > x, 16.0, count)


def _stack_rows(rows):
    n = len(rows)
    ri = lax.broadcasted_iota(jnp.int32, (n, rows[0].shape[1]), 0)
    out = jnp.broadcast_to(rows[0], ri.shape)
    for r in range(1, n):
        out = jnp.where(ri == r, rows[r], out)
    return out


def _route_kernel(ht_ref, wqt_ref, sk_ref, na_ref, e1_ref, rb_ref, e2_ref, q_ref, sc_ref):
    K = PEER_TOPK
    q_ref[...] = _dot(wqt_ref[...], ht_ref[...]).astype(BF16)
    for h in range(PEER_HEADS):
        sc_ref[h] = _dot(sk_ref[h], q_ref[h * PEER_DQ:(h + 1) * PEER_DQ, :])

    n_groups = ht_ref.shape[1] // LANES

    def one_group(h, g):
        lanes = pl.ds(pl.multiple_of(g * LANES, LANES), LANES)
        s1 = sc_ref[h, :PEER_NKEYS, lanes]
        s2 = sc_ref[h, PEER_NKEYS:, lanes]
        a = _top_values(s1)
        b = _top_values(s2)
        rb = _count_greater(b, s2)
        amat = _stack_rows(a)
        irow = lax.broadcasted_iota(jnp.int32, amat.shape, 0)
        lists = []
        for jj in range(K):
            col = amat + b[jj]
            lists.append(jnp.where((irow + 1) * (jj + 1) <= K, col, -jnp.inf))
        vmax = a[0] + b[0]
        z = jnp.zeros_like(vmax)
        cnt = jnp.zeros(amat.shape, F32)
        for r in range(K):
            m = jnp.max(lists[0], axis=0, keepdims=True)
            z = z + jnp.exp(m - vmax)
            drawn = lists[0] == m
            cnt = cnt + jnp.where(drawn, 1.0, 0.0)
            for kk in range(K - 1 - r):
                lists[kk] = jnp.where(drawn, lists[kk + 1], lists[kk])
        counts = [cnt[i:i + 1] for i in range(8)]
        n_tail = jnp.sum(cnt[8:16], axis=0, keepdims=True)
        low = jnp.full_like(n_tail, jnp.inf)
        for k in range(1, 9):
            low = jnp.where(n_tail == float(k), a[7 + k], low)
        na = jnp.where(s1 >= low, jnp.where(s1 < a[7], 1.0, 0.0), 0.0)
        for i in range(8):
            na = jnp.where(s1 == a[i], counts[i], na)
        na_ref[h, :, lanes] = na
        e1_ref[h, :, lanes] = jnp.exp(s1 - a[0])
        rb_ref[h, :, lanes] = rb.astype(BF16)
        e2_ref[h, :, lanes] = (jnp.exp(s2 - b[0]) * (0.5 / z)).astype(BF16)

    per_head = n_groups // ROUTE_UNROLL

    def group(idx, _):
        h = idx // per_head
        g0 = (idx % per_head) * ROUTE_UNROLL
        for k in range(ROUTE_UNROLL):
            one_group(h, g0 + k)
        return 0

    lax.fori_loop(0, PEER_HEADS * per_head, group, 0)


def _route(h1t, wqt, skbd):
    D, T = h1t.shape
    tm = ROUTE_TM
    full = lambda a: pl.BlockSpec(a.shape, lambda i: (0,) * a.ndim)
    ospec = pl.BlockSpec((PEER_HEADS, PEER_NKEYS, tm), lambda i: (0, 0, i))
    oshape = lambda dt: jax.ShapeDtypeStruct((PEER_HEADS, PEER_NKEYS, T), dt)
    return pl.pallas_call(
        _route_kernel,
        grid=(T // tm,),
        in_specs=[pl.BlockSpec((D, tm), lambda i: (0, i)), full(wqt), full(skbd)],
        out_specs=[ospec, ospec, ospec, ospec],
        out_shape=[oshape(F32), oshape(F32), oshape(BF16), oshape(BF16)],
        scratch_shapes=[pltpu.VMEM((PEER_HEADS * PEER_DQ, tm), BF16),
                        pltpu.VMEM((PEER_HEADS, 2 * PEER_NKEYS, tm), F32)],
        compiler_params=pltpu.CompilerParams(
            dimension_semantics=("arbitrary",), vmem_limit_bytes=VMEM_LIMIT),
        name="route",
    )(h1t, wqt, skbd)


EXP_TM = 512
EXP_TE = 2048
EXP_IA = EXP_TE // PEER_NKEYS
BF16_ROWS = 16
SQRT_HALF = 0.7071067811865476


def _expert_kernel(xt_ref, h1_ref, u_ref, v_ref, na_ref, e1_ref, rb_ref, e2_ref, g2_ref, b2_ref,
                   o_ref, acc_ref, hu_ref, a_ref):
    j = pl.program_id(1)
    nj = pl.num_programs(1)
    tm = xt_ref.shape[1]
    nsub = PEER_NKEYS // BF16_ROWS

    @pl.when(j == 0)
    def _():
        acc_ref[...] = jnp.zeros_like(acc_ref)

    hu_ref[...] = _dot(u_ref[...], xt_ref[...])
    rows = pl.ds(pl.multiple_of(j * EXP_IA, EXP_IA), EXP_IA)
    for il in range(EXP_IA):
        sl = slice(il * PEER_NKEYS, (il + 1) * PEER_NKEYS)
        hu = hu_ref[sl, :]
        hb = hu.astype(BF16)
        act = hb * (1.0 + lax.erf(hb * SQRT_HALF))
        gate = jnp.zeros((nsub, BF16_ROWS, tm), BF16)
        for h in range(PEER_HEADS):
            na = jnp.broadcast_to(na_ref[h, rows, :][il:il + 1], (BF16_ROWS, tm)).astype(BF16)
            e1 = jnp.broadcast_to(e1_ref[h, rows, :][il:il + 1], (BF16_ROWS, tm)).astype(BF16)
            rb = rb_ref[h].reshape(nsub, BF16_ROWS, tm)
            w = e2_ref[h].reshape(nsub, BF16_ROWS, tm) * e1[None]
            gate = gate + jnp.where(rb < na[None], w, jnp.zeros_like(w))
        a_ref[sl, :] = act * gate.reshape(PEER_NKEYS, tm)
    acc_ref[...] += _dot_tn(a_ref[...], v_ref[...])

    @pl.when(j == nj - 1)
    def _():
        o_ref[...] = _ln(DEEPNORM_ALPHA * h1_ref[...] + acc_ref[...], g2_ref[...], b2_ref[...])


def _experts(h1t, h1, u, v, na, e1, rb, e2, g2, b2):
    T, D = h1.shape
    E = u.shape[0]
    tm, te = EXP_TM, EXP_TE
    gspec = pl.BlockSpec((PEER_HEADS, PEER_NKEYS, tm), lambda i, j: (0, 0, i))
    vec = pl.BlockSpec((1, D), lambda i, j: (0, 0))
    return pl.pallas_call(
        _expert_kernel,
        grid=(T // tm, E // te),
        in_specs=[
            pl.BlockSpec((D, tm), lambda i, j: (0, i)),
            pl.BlockSpec((tm, D), lambda i, j: (i, 0)),
            pl.BlockSpec((te, D), lambda i, j: (j, 0)),
            pl.BlockSpec((te, D), lambda i, j: (j, 0)),
            gspec, gspec, gspec, gspec, vec, vec,
        ],
        out_specs=pl.BlockSpec((tm, D), lambda i, j: (i, 0)),
        out_shape=jax.ShapeDtypeStruct((T, D), F32),
        scratch_shapes=[pltpu.VMEM((tm, D), F32), pltpu.VMEM((te, tm), F32), pltpu.VMEM((te, tm), BF16)],
        compiler_params=pltpu.CompilerParams(
            dimension_semantics=("arbitrary", "arbitrary"), vmem_limit_bytes=VMEM_LIMIT),
        name="experts",
    )(h1t, h1, u, v, na, e1, rb, e2, g2, b2)


def kernel(x, meta_tokens, emb_ln_g, emb_ln_b, rel_bias, w_in, b_in, attn_sinks, w_pa, w_pb, w_o,
           ln1_g, ln1_b, peer_wq, peer_subkeys, peer_u, peer_v, ln2_g, ln2_b):
    B, S, D = x.shape
    T = B * S
    row = lambda a: a.reshape(1, -1).astype(F32)

    w, bi = w_in[0], b_in[0]
    sizes = (W_QA, W_KA, W_KA, W_B, W_B, W_B, N_HEADS_B, D, D)
    offs = np.cumsum((0,) + sizes)
    col = lambda a, k: a[..., offs[k]:offs[k + 1]]
    scale = HEAD_DIM ** -0.5 * LOG2E
    wq = jnp.concatenate([col(w, 0) * scale, col(w, 3) * scale, col(w, 7), col(w, 8)], axis=-1).astype(BF16)
    bq = jnp.concatenate([col(bi, 0) * scale, col(bi, 3) * scale, col(bi, 7), col(bi, 8)], axis=-1)
    dup = lambda a: jnp.repeat(a.reshape(a.shape[:-1] + (N_KV_A, 1, HEAD_DIM)), 2, axis=-2).reshape(
        a.shape[:-1] + (2 * W_KA,))
    wkv = jnp.concatenate([dup(col(w, 1)), dup(col(w, 2)), col(w, 4)], axis=-1).astype(BF16)
    bkv = jnp.concatenate([dup(col(bi, 1)), dup(col(bi, 2)), col(bi, 4)], axis=-1)
    wvbt = col(w, 5).T.astype(BF16)
    bvbt = col(bi, 5).reshape(-1, 1).astype(F32)
    fpad = LANES - N_HEADS_B
    wfb = jnp.pad(col(w, 6), ((0, 0), (0, fpad))).astype(BF16)
    bfb = jnp.pad(col(bi, 6), (0, fpad))

    g0, b0 = row(emb_ln_g), row(emb_ln_b)
    kva, kb, vbt, c, zq = _inproj(x, meta_tokens.astype(F32), g0, b0, wkv, row(bkv), wvbt, bvbt,
                                  wfb, row(bfb), wq, row(bq))
    oa = _swa(zq, kva, rel_bias.astype(F32), attn_sinks[0].astype(F32))
    ob = _fox(zq, kb, vbt, c)

    h1, h1t = _merge(x.reshape(T, D), oa.reshape(T, W_QA), ob.reshape(T, W_B), zq.reshape(T, -1),
                     g0, b0, w_pa[0].astype(BF16), w_pb[0].astype(BF16), w_o[0].astype(BF16),
                     row(ln1_g[0]), row(ln1_b[0]))

    wqt = peer_wq[0].T.astype(BF16)
    sk = peer_subkeys[0].astype(BF16)
    zero = jnp.zeros_like(sk[:, 0])
    skbd = jnp.concatenate([jnp.concatenate([sk[:, 0], zero], axis=-1),
                            jnp.concatenate([zero, sk[:, 1]], axis=-1)], axis=1)
    na, e1, rb, e2 = _route(h1t, wqt, skbd)
    out = _experts(h1t, h1, peer_u[0].astype(BF16), peer_v[0].astype(BF16), na, e1, rb, e2,
                   row(ln2_g[0]), row(ln2_b[0]))
    return out.reshape(B, S, D)
```
